```python
import math
import jax
import jax.numpy as jnp
from jax import lax
import numpy as np


D_MODEL = 4096
BATCH = 8
SEQ = 2048
DEPTH = 2
DEC_BATCH = 1
DEC_SEQ = 8192
PAST_LEN = 128

EPS = 1e-6
NEG = -1e30
ROPE_THETA = 10000.0
MIX = D_MODEL
GROUP_W = (3 * MIX) // 8
HY_CH = MIX - 2 * GROUP_W
HY_EMB = 33
HY_BANDS = (HY_EMB - 1) // 2
HY_FFN = 64
HY_FAST_DECAY = 0.3
HY_SLOW_DECAY = 1.5
HY_TARGET = 1e-2
Q_LORA = 1536
KV_LORA = 512
NOPE_DIM = 128
ROPE_DIM = 64
V_DIM = 128
QK_DIM = NOPE_DIM + ROPE_DIM
MLA_HEADS = GROUP_W // V_DIM
ATT_BLOCK = 128
DIL_DIM = 128
DIL_PAIRS = ((128, 1), (512, 4), (2048, 16))
DIL_HEADS = GROUP_W // DIL_DIM
DIL_SLOTS = DIL_HEADS // len(DIL_PAIRS)
DIL_BLOCK = 64
D_FF = 4 * D_MODEL
HY_COLS = 3 * HY_CH
MLA_COLS = Q_LORA + KV_LORA + ROPE_DIM
DIL_COLS = 3 * DIL_HEADS * DIL_DIM
IN_COLS = HY_COLS + MLA_COLS + DIL_COLS

kernel_name = 'hybrid_hyena_mla_dilated_encoder'


def rmsnorm(x, g):
    xf = x.astype(jnp.float32)
    y = xf * lax.rsqrt(jnp.mean(xf * xf, axis=-1, keepdims=True) + EPS)
    return (y * g.astype(jnp.float32)).astype(x.dtype)


def rope_tables(L, dim):
    inv = 1.0 / (ROPE_THETA ** (jnp.arange(0, dim, 2, dtype=jnp.float32) / dim))
    ang = jnp.arange(L, dtype=jnp.float32)[:, None] * inv[None, :]
    return jnp.cos(ang), jnp.sin(ang)


def apply_rope(x, cos, sin):
    x1, x2 = jnp.split(x.astype(jnp.float32), 2, axis=-1)
    c = cos[None, :, None, :]
    s = sin[None, :, None, :]
    return jnp.concatenate([x1 * c - x2 * s, x2 * c + x1 * s], axis=-1).astype(x.dtype)


def hyena_filter(L, w1, b1, w2, b2, w3, b3, w4, b4, freq):
    t = jnp.linspace(0.0, 1.0, L, dtype=jnp.float32)[:, None]
    w = 2.0 * math.pi * jnp.arange(L, dtype=jnp.float32)[:, None] / L
    f = jnp.linspace(1e-4, HY_BANDS - 1, HY_BANDS, dtype=jnp.float32)[None, :]
    z = jnp.concatenate([t, jnp.cos(f * w), -jnp.sin(f * w)], axis=-1)
    fq = freq.astype(jnp.float32)
    h = jnp.sin(fq * (z @ w1.astype(jnp.float32) + b1.astype(jnp.float32)))
    h = jnp.sin(fq * (h @ w2.astype(jnp.float32) + b2.astype(jnp.float32)))
    h = jnp.sin(fq * (h @ w3.astype(jnp.float32) + b3.astype(jnp.float32)))
    h = h @ w4.astype(jnp.float32) + b4.astype(jnp.float32)
    max_decay = math.log(HY_TARGET) / HY_FAST_DECAY
    min_decay = math.log(HY_TARGET) / HY_SLOW_DECAY
    deltas = jnp.abs(jnp.linspace(min_decay, max_decay, HY_CH, dtype=jnp.float32))
    decay = jnp.exp(-t * deltas[None, :])
    return h * jnp.concatenate([decay, decay], axis=-1)


def hyena_mixer(z, conv_w, conv_b, w1, b1, w2, b2, w3, b3, w4, b4, freq, skip):
    B, L, _ = z.shape
    zp = jnp.pad(z, ((0, 0), (1, 1), (0, 0)))
    zc = zp[:, :-2] * conv_w[0] + zp[:, 1:-1] * conv_w[1] + zp[:, 2:] * conv_w[2] + conv_b
    x0, x1, v = jnp.split(zc, 3, axis=-1)
    u = (x1 * v).astype(jnp.float32)
    h = hyena_filter(L, w1, b1, w2, b2, w3, b3, w4, b4, freq)
    h_f, h_b = h[:, :HY_CH], h[:, HY_CH:]
    k_circ = jnp.concatenate([h_f, jnp.zeros((1, HY_CH), jnp.float32), h_b[:0:-1]], axis=0)
    uf = jnp.fft.rfft(u, n=2 * L, axis=1)
    kf = jnp.fft.rfft(k_circ, n=2 * L, axis=0)
    y = jnp.fft.irfft(uf * kf[None], n=2 * L, axis=1)[:, :L]
    y = y + u * skip.astype(jnp.float32)
    return x0 * y.astype(z.dtype)


def mla_mixer(cols, q_a_norm, w_q_b, kv_a_norm, w_kv_b, qn_nope, qn_rope, kn_nope, kn_rope):
    B, L, _ = cols.shape
    c_q = rmsnorm(cols[..., :Q_LORA], q_a_norm)
    c_kv = rmsnorm(cols[..., Q_LORA:Q_LORA + KV_LORA], kv_a_norm)
    k_rope = rmsnorm(cols[..., Q_LORA + KV_LORA:], kn_rope)[:, :, None, :]
    q = jnp.einsum('blr,rf->blf', c_q, w_q_b).reshape(B, L, MLA_HEADS, QK_DIM)
    kv = jnp.einsum('blr,rf->blf', c_kv, w_kv_b).reshape(B, L, MLA_HEADS, NOPE_DIM + V_DIM)
    q_nope = rmsnorm(q[..., :NOPE_DIM], qn_nope)
    q_rope = rmsnorm(q[..., NOPE_DIM:], qn_rope)
    k_nope = rmsnorm(kv[..., :NOPE_DIM], kn_nope)
    v = kv[..., NOPE_DIM:]
    cos, sin = rope_tables(L, ROPE_DIM)
    q_rope = apply_rope(q_rope, cos, sin)
    k_rope = apply_rope(k_rope, cos, sin)[:, :, 0, :]
    nq = L // ATT_BLOCK
    scale = QK_DIM ** -0.5

    def blocks(t):
        return jnp.moveaxis(t.reshape(B, nq, ATT_BLOCK, *t.shape[2:]), 1, 0)

    def one_block(args):
        qn, qr = args
        s = (jnp.einsum('bqhd,bkhd->bhqk', qn, k_nope, preferred_element_type=jnp.float32)
             + jnp.einsum('bqhd,bkd->bhqk', qr, k_rope, preferred_element_type=jnp.float32)) * scale
        p = jax.nn.softmax(s, axis=-1)
        return jnp.einsum('bhqk,bkhd->bqhd', p.astype(v.dtype), v)

    o = lax.map(one_block, (blocks(q_nope), blocks(q_rope)))
    return jnp.moveaxis(o, 0, 1).reshape(B, L, MLA_HEADS * V_DIM)


def dilated_branch(q, k, v, dil, radius):
    B, L, h, dh = q.shape
    n = L // dil
    nblk = -(-n // DIL_BLOCK)
    npad = nblk * DIL_BLOCK
    BD = B * dil

    def by_stride(t):
        t = t.reshape(B, n, dil, h, dh).transpose(0, 2, 1, 3, 4).reshape(BD, n, h, dh)
        return jnp.pad(t, ((0, 0), (0, npad - n), (0, 0), (0, 0)))

    def band(t):
        tp = jnp.pad(t, ((0, 0), (DIL_BLOCK, DIL_BLOCK), (0, 0), (0, 0))).reshape(BD, nblk + 2, DIL_BLOCK, h, dh)
        return jnp.concatenate([tp[:, :-2], tp[:, 1:-1], tp[:, 2:]], axis=2)

    qb = by_stride(q).reshape(BD, nblk, DIL_BLOCK, h, dh)
    kb = band(by_stride(k))
    vb = band(by_stride(v))
    s = jnp.einsum('bnqhd,bnkhd->bnhqk', qb, kb, preferred_element_type=jnp.float32) * (dh ** -0.5)
    qpos = jnp.arange(nblk)[:, None, None] * DIL_BLOCK + jnp.arange(DIL_BLOCK)[None, :, None]
    kpos = (jnp.arange(nblk)[:, None, None] - 1) * DIL_BLOCK + jnp.arange(3 * DIL_BLOCK)[None, None, :]
    valid = (jnp.abs(qpos - kpos) <= radius) & (kpos >= 0) & (kpos < n)
    s = jnp.where(valid[None, :, None], s, NEG)
    m = jnp.max(s, axis=-1, keepdims=True)
    p = jnp.exp(s - m)
    l = jnp.sum(p, axis=-1, keepdims=True)
    o = jnp.einsum('bnhqk,bnkhd->bnqhd', (p / l).astype(v.dtype), vb)
    lse = (m + jnp.log(l))[..., 0]
    o = o.reshape(B, dil, npad, h, dh)[:, :, :n].transpose(0, 2, 1, 3, 4).reshape(B, L, h, dh)
    lse = lse.transpose(0, 1, 3, 2).reshape(B, dil, npad, h)[:, :, :n].transpose(0, 2, 1, 3).reshape(B, L, h)
    return o, lse


def dilated_mixer(cols, q_norm, k_norm):
    B, L, _ = cols.shape
    q, k, v = [t.reshape(B, L, DIL_HEADS, DIL_DIM) for t in jnp.split(cols, 3, axis=-1)]
    q = rmsnorm(q, q_norm)
    k = rmsnorm(k, k_norm)
    cos, sin = rope_tables(L, DIL_DIM)
    q = apply_rope(q, cos, sin)
    k = apply_rope(k, cos, sin)
    outs, lses = [], []
    for g, (window, dil) in enumerate(DIL_PAIRS):
        sl = slice(g * DIL_SLOTS, (g + 1) * DIL_SLOTS)
        o, lse = dilated_branch(q[:, :, sl], k[:, :, sl], v[:, :, sl], dil, window // (2 * dil))
        outs.append(o)
        lses.append(lse)
    o = jnp.stack(outs, axis=2)
    alpha = jax.nn.softmax(jnp.stack(lses, axis=2), axis=2)
    o = o * alpha[..., None].astype(o.dtype)
    return o.reshape(B, L, DIL_HEADS * DIL_DIM)


def setup_inputs(seed: int = 0):
    key = jax.random.key(seed)
    ks = iter(jax.random.split(key, 40))

    def nrm(shape, scale):
        return jax.random.normal(next(ks), shape, jnp.float32) * scale

    def gain(shape):
        return 1.0 + 0.05 * jax.random.normal(next(ks), shape, jnp.float32)

    Ld = DEPTH
    return {
        'x_prompt': nrm((BATCH, SEQ, D_MODEL), 1.0),
        'x_sample': nrm((DEC_BATCH, DEC_SEQ, D_MODEL), 1.0),
        'norm_mix': gain((Ld, D_MODEL)),
        'w_in': nrm((Ld, D_MODEL, IN_COLS), D_MODEL ** -0.5),
        'hy_conv_w': nrm((Ld, 3, HY_COLS), 3 ** -0.5),
        'hy_conv_b': nrm((Ld, HY_COLS), 0.02),
        'hy_f_w1': nrm((Ld, HY_EMB, HY_FFN), HY_EMB ** -0.5),
        'hy_f_b1': nrm((Ld, HY_FFN), 0.02),
        'hy_f_w2': nrm((Ld, HY_FFN, HY_FFN), HY_FFN ** -0.5),
        'hy_f_b2': nrm((Ld, HY_FFN), 0.02),
        'hy_f_w3': nrm((Ld, HY_FFN, HY_FFN), HY_FFN ** -0.5),
        'hy_f_b3': nrm((Ld, HY_FFN), 0.02),
        'hy_f_w4': nrm((Ld, HY_FFN, 2 * HY_CH), HY_FFN ** -0.5),
        'hy_f_b4': nrm((Ld, 2 * HY_CH), 0.02),
        'hy_f_freq': gain((Ld, HY_FFN)),
        'hy_skip': nrm((Ld, HY_CH), 0.1),
        'mla_q_a_norm': gain((Ld, Q_LORA)),
        'mla_w_q_b': nrm((Ld, Q_LORA, MLA_HEADS * QK_DIM), Q_LORA ** -0.5),
        'mla_kv_a_norm': gain((Ld, KV_LORA)),
        'mla_w_kv_b': nrm((Ld, KV_LORA, MLA_HEADS * (NOPE_DIM + V_DIM)), KV_LORA ** -0.5),
        'mla_qn_nope': gain((Ld, NOPE_DIM)),
        'mla_qn_rope': gain((Ld, ROPE_DIM)),
        'mla_kn_nope': gain((Ld, NOPE_DIM)),
        'mla_kn_rope': gain((Ld, ROPE_DIM)),
        'dil_q_norm': gain((Ld, DIL_DIM)),
        'dil_k_norm': gain((Ld, DIL_DIM)),
        'out_norm': gain((Ld, MIX)),
        'w_out': nrm((Ld, MIX, D_MODEL), MIX ** -0.5),
        'norm_ffn': gain((Ld, D_MODEL)),
        'w_up': nrm((Ld, D_MODEL, D_FF), D_MODEL ** -0.5),
        'w_down': nrm((Ld, D_FF, D_MODEL), D_FF ** -0.5),
    }


def reference(x_prompt, x_sample, norm_mix, w_in, hy_conv_w, hy_conv_b, hy_f_w1, hy_f_b1, hy_f_w2, hy_f_b2,
              hy_f_w3, hy_f_b3, hy_f_w4, hy_f_b4, hy_f_freq, hy_skip, mla_q_a_norm, mla_w_q_b, mla_kv_a_norm,
              mla_w_kv_b, mla_qn_nope, mla_qn_rope, mla_kn_nope, mla_kn_rope, dil_q_norm, dil_k_norm,
              out_norm, w_out, norm_ffn, w_up, w_down):
    def trunk(x):
        for l in range(DEPTH):
            h = rmsnorm(x, norm_mix[l])
            z = jnp.einsum('bld,df->blf', h, w_in[l])
            y_hy = hyena_mixer(z[..., :HY_COLS], hy_conv_w[l], hy_conv_b[l], hy_f_w1[l], hy_f_b1[l],
                               hy_f_w2[l], hy_f_b2[l], hy_f_w3[l], hy_f_b3[l], hy_f_w4[l], hy_f_b4[l],
                               hy_f_freq[l], hy_skip[l])
            y_mla = mla_mixer(z[..., HY_COLS:HY_COLS + MLA_COLS], mla_q_a_norm[l], mla_w_q_b[l],
                              mla_kv_a_norm[l], mla_w_kv_b[l], mla_qn_nope[l], mla_qn_rope[l],
                              mla_kn_nope[l], mla_kn_rope[l])
            y_dil = dilated_mixer(z[..., HY_COLS + MLA_COLS:], dil_q_norm[l], dil_k_norm[l])
            g = out_norm[l]
            mixed = jnp.concatenate([rmsnorm(y_hy, g[:HY_CH]),
                                     rmsnorm(y_mla, g[HY_CH:HY_CH + GROUP_W]),
                                     rmsnorm(y_dil, g[HY_CH + GROUP_W:])], axis=-1)
            x = x + jnp.einsum('blm,md->bld', mixed, w_out[l])
            h = rmsnorm(x, norm_ffn[l])
            a = jax.nn.relu(jnp.einsum('bld,df->blf', h, w_up[l]))
            x = x + jnp.einsum('blf,fd->bld', a * a, w_down[l])
        return x

    y_prompt = trunk(x_prompt)
    y_sample = trunk(x_sample)
    return (y_prompt, y_sample)
```

```python
import functools
import math

import numpy as np
import jax
import jax.numpy as jnp
from jax import lax
from jax.experimental import pallas as pl
from jax.experimental.pallas import tpu as pltpu

F32 = jnp.float32
BF16 = jnp.bfloat16

EPS = 1e-6
NEG = -1e30
ROPE_THETA = 10000.0
LANES = 128
VMEM_LIMIT = 56 * 1024 * 1024

HY_CH = 1024
GROUP_W = 1536
HEADS = 12
HEAD_DIM = 128
ROPE_DIM = 64
QK_DIM = HEAD_DIM + ROPE_DIM
Q_LORA = 1536
KV_LORA = 512
DIL_PAIRS = ((128, 1), (512, 4), (2048, 16))
DIL_SLOTS = 4
HY_EMB = 33
HY_BANDS = 16
HY_FFN = 64

COL_DQ, COL_DK, COL_DV = 0, 1536, 3072
COL_CQ = 4608
COL_HY = 6144
COL_CKV = 9216
COL_KR = 9728
Z_COLS = 9856
DFT_Q = 128


def _params(sem, vmem=VMEM_LIMIT):
    return pltpu.CompilerParams(dimension_semantics=sem, vmem_limit_bytes=vmem)


def _rms_scale(x, width):
    return lax.rsqrt(jnp.sum(x * x, axis=-1, keepdims=True) * (1.0 / width) + EPS)


def _norm_matmul_kernel(x_ref, g_ref, w_ref, o_ref, h_ref):
    @pl.when(pl.program_id(1) == 0)
    def _():
        x = x_ref[...]
        h_ref[...] = (x * _rms_scale(x, x.shape[-1]) * g_ref[...]).astype(BF16)

    o_ref[...] = jnp.dot(h_ref[...], w_ref[...], preferred_element_type=F32)


def norm_matmul(x, g, w, tm=512, tn=896):
    m, d = x.shape
    n = w.shape[1]
    return pl.pallas_call(
        _norm_matmul_kernel,
        out_shape=jax.ShapeDtypeStruct((m, n), F32),
        grid=(m // tm, n // tn),
        in_specs=[pl.BlockSpec((tm, d), lambda i, j: (i, 0)),
                  pl.BlockSpec((1, d), lambda i, j: (0, 0)),
                  pl.BlockSpec((d, tn), lambda i, j: (0, j))],
        out_specs=pl.BlockSpec((tm, tn), lambda i, j: (i, j)),
        scratch_shapes=[pltpu.VMEM((tm, d), BF16)],
        compiler_params=_params(("parallel", "arbitrary")),
        name="norm_matmul",
    )(x, g.reshape(1, d), w)


def _out_proj_kernel(hy_ref, mla_ref, d0_ref, d1_ref, d2_ref, g_ref, w_ref, x_ref, o_ref, h_ref):
    @pl.when(pl.program_id(1) == 0)
    def _():
        hy = hy_ref[...]
        h_ref[:, :HY_CH] = (hy * _rms_scale(hy, HY_CH) * g_ref[:, :HY_CH]).astype(BF16)
        ml = mla_ref[...]
        h_ref[:, HY_CH:HY_CH + GROUP_W] = (
            ml * _rms_scale(ml, GROUP_W) * g_ref[:, HY_CH:HY_CH + GROUP_W]).astype(BF16)
        d0, d1, d2 = d0_ref[...], d1_ref[...], d2_ref[...]
        ss = (jnp.sum(d0 * d0, axis=-1, keepdims=True) + jnp.sum(d1 * d1, axis=-1, keepdims=True)
              + jnp.sum(d2 * d2, axis=-1, keepdims=True))
        r = lax.rsqrt(ss * (1.0 / GROUP_W) + EPS)
        base = HY_CH + GROUP_W
        gw = GROUP_W // 3
        for k, dk in enumerate((d0, d1, d2)):
            lo = base + k * gw
            h_ref[:, lo:lo + gw] = (dk * r * g_ref[:, lo:lo + gw]).astype(BF16)

    o_ref[...] = x_ref[...] + jnp.dot(h_ref[...], w_ref[...], preferred_element_type=F32)


def out_proj(y_hy, y_mla, y_d0, y_d1, y_d2, g, w, x, tm=512, tn=1024):
    m, d = x.shape
    mix = w.shape[0]
    gw = GROUP_W // 3
    row = lambda i, j: (i, 0)
    return pl.pallas_call(
        _out_proj_kernel,
        out_shape=jax.ShapeDtypeStruct((m, d), F32),
        grid=(m // tm, d // tn),
        in_specs=[pl.BlockSpec((tm, HY_CH), row),
                  pl.BlockSpec((tm, GROUP_W), row),
                  pl.BlockSpec((tm, gw), row),
                  pl.BlockSpec((tm, gw), row),
                  pl.BlockSpec((tm, gw), row),
                  pl.BlockSpec((1, mix), lambda i, j: (0, 0)),
                  pl.BlockSpec((mix, tn), lambda i, j: (0, j)),
                  pl.BlockSpec((tm, tn), lambda i, j: (i, j))],
        out_specs=pl.BlockSpec((tm, tn), lambda i, j: (i, j)),
        scratch_shapes=[pltpu.VMEM((tm, mix), BF16)],
        compiler_params=_params(("parallel", "arbitrary")),
        name="out_proj",
    )(y_hy, y_mla, y_d0, y_d1, y_d2, g.reshape(1, mix), w, x)


def _ffn_kernel(x_ref, g_ref, wu_ref, wd_ref, o_ref, h_ref):
    @pl.when(pl.program_id(1) == 0)
    def _():
        x = x_ref[...]
        h_ref[...] = (x * _rms_scale(x, x.shape[-1]) * g_ref[...]).astype(BF16)
        o_ref[...] = x

    a = jnp.maximum(jnp.dot(h_ref[...], wu_ref[...], preferred_element_type=F32), 0.0)
    o_ref[...] += jnp.dot((a * a).astype(BF16), wd_ref[...], preferred_element_type=F32)


def ffn(x, g, w_up, w_down, tm=512, tf=256):
    m, d = x.shape
    f = w_up.shape[1]
    return pl.pallas_call(
        _ffn_kernel,
        out_shape=jax.ShapeDtypeStruct((m, d), F32),
        grid=(m // tm, f // tf),
        in_specs=[pl.BlockSpec((tm, d), lambda i, j: (i, 0)),
                  pl.BlockSpec((1, d), lambda i, j: (0, 0)),
                  pl.BlockSpec((d, tf), lambda i, j: (0, j)),
                  pl.BlockSpec((tf, d), lambda i, j: (j, 0))],
        out_specs=pl.BlockSpec((tm, d), lambda i, j: (i, 0)),
        scratch_shapes=[pltpu.VMEM((tm, d), BF16)],
        compiler_params=_params(("parallel", "arbitrary")),
        name="ffn",
    )(x, g.reshape(1, d), w_up, w_down)


def _rope_tables(pos, dim):
    inv = 1.0 / (ROPE_THETA ** (jnp.arange(0, dim, 2, dtype=F32) / dim))
    ang = pos.astype(F32)[:, None] * inv[None, :]
    return jnp.cos(ang), jnp.sin(ang)


def _spread_halves(a):
    z = jnp.zeros(a.shape[:-1] + (ROPE_DIM // 2,), a.dtype)
    return jnp.concatenate([a[..., :ROPE_DIM // 2], z, a[..., ROPE_DIM // 2:], z], axis=-1)


def _rotate_half(x, cos, sin_signed):
    return x * cos + pltpu.roll(x, LANES // 2, axis=1) * sin_signed


def _mla_q_kernel(cq_ref, ga_ref, w_ref, gn_ref, cos_ref, sin_ref, q_ref, c_ref):
    @pl.when(pl.program_id(1) == 0)
    def _():
        c = cq_ref[...]
        c_ref[...] = (c * _rms_scale(c, Q_LORA) * ga_ref[...]).astype(BF16)

    q = jnp.dot(c_ref[...], w_ref[...], preferred_element_type=F32)
    scale = QK_DIM ** -0.5
    qn = q[:, :HEAD_DIM]
    qn = qn * _rms_scale(qn, HEAD_DIM) * gn_ref[:, :HEAD_DIM]
    qr = q[:, HEAD_DIM:]
    qr = qr * _rms_scale(qr, ROPE_DIM) * gn_ref[:, HEAD_DIM:]
    qr = _rotate_half(qr, cos_ref[...], sin_ref[...])
    q_ref[:, :HEAD_DIM] = (qn * scale).astype(BF16)
    q_ref[:, HEAD_DIM:] = (qr * scale).astype(BF16)


def mla_q(z, ga, w, gn, cos, sin, tm=512):
    m = z.shape[0]
    return pl.pallas_call(
        _mla_q_kernel,
        out_shape=jax.ShapeDtypeStruct((HEADS, m, 2 * LANES), BF16),
        grid=(m // tm, HEADS),
        in_specs=[pl.BlockSpec((tm, Q_LORA), lambda i, h: (i, COL_CQ // Q_LORA)),
                  pl.BlockSpec((1, Q_LORA), lambda i, h: (0, 0)),
                  pl.BlockSpec((None, Q_LORA, 2 * LANES), lambda i, h: (h, 0, 0)),
                  pl.BlockSpec((1, 2 * LANES), lambda i, h: (0, 0)),
                  pl.BlockSpec((tm, LANES), lambda i, h: (i, 0)),
                  pl.BlockSpec((tm, LANES), lambda i, h: (i, 0))],
        out_specs=pl.BlockSpec((None, tm, 2 * LANES), lambda i, h: (h, i, 0)),
        scratch_shapes=[pltpu.VMEM((tm, Q_LORA), BF16)],
        compiler_params=_params(("parallel", "arbitrary")),
        name="mla_q",
    )(z, ga, w, gn, cos, sin)


def _mla_kv_kernel(ckv_ref, kr_ref, ga_ref, w_ref, gk_ref, gr_ref, cos_ref, sin_ref,
                   k_ref, v_ref, c_ref, r_ref):
    @pl.when(pl.program_id(1) == 0)
    def _():
        c = ckv_ref[...]
        c_ref[...] = (c * _rms_scale(c, KV_LORA) * ga_ref[...]).astype(BF16)
        kr = kr_ref[...]
        kr = kr * _rms_scale(kr, ROPE_DIM) * gr_ref[...]
        r_ref[...] = _rotate_half(kr, cos_ref[...], sin_ref[...]).astype(BF16)

    kv = jnp.dot(c_ref[...], w_ref[...], preferred_element_type=F32)
    kn = kv[:, :HEAD_DIM]
    k_ref[:, :HEAD_DIM] = (kn * _rms_scale(kn, HEAD_DIM) * gk_ref[...]).astype(BF16)
    k_ref[:, HEAD_DIM:] = r_ref[...]
    v_ref[...] = kv[:, HEAD_DIM:].astype(BF16)


def mla_kv(z, ga, w, gk, gr, cos, sin, tm=512):
    m = z.shape[0]
    return pl.pallas_call(
        _mla_kv_kernel,
        out_shape=(jax.ShapeDtypeStruct((HEADS, m, 2 * LANES), BF16),
                   jax.ShapeDtypeStruct((HEADS, m, LANES), BF16)),
        grid=(m // tm, HEADS),
        in_specs=[pl.BlockSpec((tm, KV_LORA), lambda i, h: (i, COL_CKV // KV_LORA)),
                  pl.BlockSpec((tm, LANES), lambda i, h: (i, COL_KR // LANES)),
                  pl.BlockSpec((1, KV_LORA), lambda i, h: (0, 0)),
                  pl.BlockSpec((None, KV_LORA, 2 * LANES), lambda i, h: (h, 0, 0)),
                  pl.BlockSpec((1, LANES), lambda i, h: (0, 0)),
                  pl.BlockSpec((1, LANES), lambda i, h: (0, 0)),
                  pl.BlockSpec((tm, LANES), lambda i, h: (i, 0)),
                  pl.BlockSpec((tm, LANES), lambda i, h: (i, 0))],
        out_specs=(pl.BlockSpec((None, tm, 2 * LANES), lambda i, h: (h, i, 0)),
                   pl.BlockSpec((None, tm, LANES), lambda i, h: (h, i, 0))),
        scratch_shapes=[pltpu.VMEM((tm, KV_LORA), BF16), pltpu.VMEM((tm, LANES), BF16)],
        compiler_params=_params(("parallel", "arbitrary")),
        name="mla_kv",
    )(z, z, ga, w, gk, gr, cos, sin)


def _flash_kernel(q_ref, k_ref, v_ref, o_ref, m_ref, l_ref, acc_ref, *, nk):
    j = pl.program_id(3)

    @pl.when(j == 0)
    def _():
        m_ref[...] = jnp.full(m_ref.shape, -jnp.inf, F32)
        l_ref[...] = jnp.zeros(l_ref.shape, F32)
        acc_ref[...] = jnp.zeros(acc_ref.shape, F32)

    s = lax.dot_general(q_ref[...], k_ref[...], (((1,), (1,)), ((), ())),
                        preferred_element_type=F32)
    m_prev = m_ref[...]
    m_new = jnp.maximum(m_prev, jnp.max(s, axis=-1, keepdims=True))
    alpha = jnp.exp(m_prev - m_new)
    p = jnp.exp(s - m_new)
    l_ref[...] = alpha * l_ref[...] + jnp.sum(p, axis=-1, keepdims=True)
    acc_ref[...] = alpha * acc_ref[...] + jnp.dot(p.astype(BF16), v_ref[...],
                                                  preferred_element_type=F32)
    m_ref[...] = m_new

    @pl.when(j == nk - 1)
    def _():
        o_ref[...] = acc_ref[...] / l_ref[...]


def mla_attention(q, k, v, batch, seq, tq=512, tk=2048):
    tq, tk = min(tq, seq), min(tk, seq)
    nq, nk = seq // tq, seq // tk
    return pl.pallas_call(
        functools.partial(_flash_kernel, nk=nk),
        out_shape=jax.ShapeDtypeStruct((batch * seq, GROUP_W), F32),
        grid=(batch, HEADS, nq, nk),
        in_specs=[pl.BlockSpec((None, tq, 2 * LANES), lambda b, h, i, j: (h, b * nq + i, 0)),
                  pl.BlockSpec((None, tk, 2 * LANES), lambda b, h, i, j: (h, b * nk + j, 0)),
                  pl.BlockSpec((None, tk, LANES), lambda b, h, i, j: (h, b * nk + j, 0))],
        out_specs=pl.BlockSpec((tq, LANES), lambda b, h, i, j: (b * nq + i, h)),
        scratch_shapes=[pltpu.VMEM((tq, 1), F32), pltpu.VMEM((tq, 1), F32),
                        pltpu.VMEM((tq, LANES), F32)],
        compiler_params=_params(("parallel", "parallel", "parallel", "arbitrary")),
        name="mla_attention",
    )(q, k, v)


def _dil_prep_kernel(q_ref, k_ref, v_ref, gq_ref, gk_ref, cos_ref, sin_ref, qo_ref, ko_ref, vo_ref):
    cos, sin = cos_ref[...], sin_ref[...]
    scale = HEAD_DIM ** -0.5
    for h in range(HEADS):
        sl = slice(h * HEAD_DIM, (h + 1) * HEAD_DIM)
        q = q_ref[:, sl]
        q = _rotate_half(q * _rms_scale(q, HEAD_DIM) * gq_ref[...], cos, sin)
        qo_ref[:, sl] = (q * scale).astype(BF16)
        k = k_ref[:, sl]
        k = _rotate_half(k * _rms_scale(k, HEAD_DIM) * gk_ref[...], cos, sin)
        ko_ref[:, sl] = k.astype(BF16)
    vo_ref[...] = v_ref[...].astype(BF16)


def dil_prep(z, gq, gk, cos, sin, tm=512):
    m = z.shape[0]
    sec = lambda c: pl.BlockSpec((tm, GROUP_W), lambda i: (i, c // GROUP_W))
    vec = pl.BlockSpec((1, LANES), lambda i: (0, 0))
    tab = pl.BlockSpec((tm, LANES), lambda i: (i, 0))
    out = pl.BlockSpec((tm, GROUP_W), lambda i: (i, 0))
    shp = jax.ShapeDtypeStruct((m, GROUP_W), BF16)
    return pl.pallas_call(
        _dil_prep_kernel,
        out_shape=(shp, shp, shp),
        grid=(m // tm,),
        in_specs=[sec(COL_DQ), sec(COL_DK), sec(COL_DV), vec, vec, tab, tab],
        out_specs=(out, out, out),
        compiler_params=_params(("parallel",)),
        name="dil_prep",
    )(z, z, z, gq, gk, cos, sin)


def _dil_attn_kernel(*refs, seq, tq):
    q_refs, k_refs, v_refs, o_refs = refs[0:3], refs[3:6], refs[6:9], refs[9:12]
    i = pl.program_id(2)
    outs, lses = [], []
    for g, (window, dil) in enumerate(DIL_PAIRS):
        reach = window // 2
        win = min(seq, tq + 2 * reach)
        start = jnp.clip(i * tq - reach, 0, seq - win)
        start = pl.multiple_of(start, 64)
        k = k_refs[g][pl.ds(start, win), :]
        v = v_refs[g][pl.ds(start, win), :]
        s = lax.dot_general(q_refs[g][...], k, (((1,), (1,)), ((), ())),
                            preferred_element_type=F32)
        qpos = i * tq + lax.broadcasted_iota(jnp.int32, (tq, win), 0)
        kpos = start + lax.broadcasted_iota(jnp.int32, (tq, win), 1)
        diff = kpos - qpos
        valid = (jnp.abs(diff) <= reach) & ((diff & (dil - 1)) == 0)
        s = jnp.where(valid, s, NEG)
        m = jnp.max(s, axis=-1, keepdims=True)
        p = jnp.exp(s - m)
        l = jnp.sum(p, axis=-1, keepdims=True)
        o = jnp.dot(p.astype(BF16), v, preferred_element_type=F32) / l
        outs.append(o)
        lses.append(m + jnp.log(l))
    top = jnp.maximum(jnp.maximum(lses[0], lses[1]), lses[2])
    es = [jnp.exp(t - top) for t in lses]
    den = es[0] + es[1] + es[2]
    for g in range(3):
        o_refs[g][...] = outs[g] * (es[g] / den)


def dil_attention(qd, kd, vd, batch, seq, tq=256):
    tq = min(tq, seq)
    nq = seq // tq
    qs = [pl.BlockSpec((tq, HEAD_DIM), functools.partial(
        lambda b, s, i, g: (b * nq + i, g * DIL_SLOTS + s), g=g)) for g in range(3)]
    ks = [pl.BlockSpec((seq, HEAD_DIM), functools.partial(
        lambda b, s, i, g: (b, g * DIL_SLOTS + s), g=g)) for g in range(3)]
    os_ = [pl.BlockSpec((tq, HEAD_DIM), lambda b, s, i: (b * nq + i, s)) for _ in range(3)]
    shp = jax.ShapeDtypeStruct((batch * seq, DIL_SLOTS * HEAD_DIM), F32)
    return pl.pallas_call(
        functools.partial(_dil_attn_kernel, seq=seq, tq=tq),
        out_shape=(shp, shp, shp),
        grid=(batch, DIL_SLOTS, nq),
        in_specs=qs + ks + ks,
        out_specs=tuple(os_),
        compiler_params=_params(("parallel", "parallel", "arbitrary")),
        name="dil_attention",
    )(qd, qd, qd, kd, kd, kd, vd, vd, vd)


def _hy_filter_kernel(z_ref, w1_ref, b1_ref, w2_ref, b2_ref, w3_ref, b3_ref, w4_ref, b4_ref,
                      fq_ref, dl_ref, o_ref, *, tl):
    fq = fq_ref[...]
    h = z_ref[...]
    for w_ref, b_ref in ((w1_ref, b1_ref), (w2_ref, b2_ref), (w3_ref, b3_ref)):
        h = jnp.sin(fq * (jnp.dot(h.astype(BF16), w_ref[...], preferred_element_type=F32) + b_ref[...]))
    h = jnp.dot(h.astype(BF16), w4_ref[...], preferred_element_type=F32) + b4_ref[...]
    t = z_ref[:, 0:1]
    decay = jnp.exp(-t * dl_ref[...])
    o_ref[0] = h[:, :HY_CH] * decay
    n = pl.program_id(0) * tl + lax.broadcasted_iota(jnp.int32, (tl, 1), 0)
    o_ref[1] = jnp.where(n == 0, 0.0, h[:, HY_CH:] * decay)


def hy_filter(zfeat, w1, b1, w2, b2, w3, b3, w4, b4, fq, deltas, tl=512):
    seq = zfeat.shape[0]
    full = lambda a: pl.BlockSpec(a.shape, lambda i: (0,) * a.ndim)
    args = (w1, b1, w2, b2, w3, b3, w4, b4, fq, deltas)
    return pl.pallas_call(
        functools.partial(_hy_filter_kernel, tl=tl),
        out_shape=jax.ShapeDtypeStruct((2, seq, HY_CH), F32),
        grid=(seq // tl,),
        in_specs=[pl.BlockSpec((tl, LANES), lambda i: (i, 0))] + [full(a) for a in args],
        out_specs=pl.BlockSpec((2, tl, HY_CH), lambda i: (0, i, 0)),
        compiler_params=_params(("parallel",)),
        name="hy_filter",
    )(zfeat, *args)


def _hy_pre_kernel(x0_ref, x1_ref, v_ref, p0_ref, p1_ref, pv_ref, n0_ref, n1_ref, nv_ref,
                   w_ref, b_ref, x0o_ref, u_ref, *, tl, blocks_per_seq):
    i = pl.program_id(0)
    first = (i % blocks_per_seq) == 0
    last = (i % blocks_per_seq) == blocks_per_seq - 1
    row = lax.broadcasted_iota(jnp.int32, (tl, 1), 0)

    def conv(c_ref, p_ref, n_ref, part):
        c = c_ref[...]
        sl = slice(part * HY_CH, (part + 1) * HY_CH)
        prev_row = jnp.where(first, 0.0, p_ref[7:8, :])
        next_row = jnp.where(last, 0.0, n_ref[0:1, :])
        down = jnp.where(row == 0, prev_row, pltpu.roll(c, 1, axis=0))
        up = jnp.where(row == tl - 1, next_row, pltpu.roll(c, tl - 1, axis=0))
        return down * w_ref[0:1, sl] + c * w_ref[1:2, sl] + up * w_ref[2:3, sl] + b_ref[:, sl]

    x0o_ref[...] = conv(x0_ref, p0_ref, n0_ref, 0)
    u_ref[...] = conv(x1_ref, p1_ref, n1_ref, 1) * conv(v_ref, pv_ref, nv_ref, 2)


def hy_pre(z, conv_w, conv_b, seq, tl=512):
    m = z.shape[0]
    nblk = m // tl
    r8 = tl // 8
    c0 = COL_HY // HY_CH
    cur = lambda p: pl.BlockSpec((tl, HY_CH), lambda i: (i, c0 + p))
    prv = lambda p: pl.BlockSpec((8, HY_CH), lambda i: (jnp.maximum(i * r8 - 1, 0), c0 + p))
    nxt = lambda p: pl.BlockSpec((8, HY_CH), lambda i: (jnp.minimum((i + 1) * r8, m // 8 - 1), c0 + p))
    out = pl.BlockSpec((tl, HY_CH), lambda i: (i, 0))
    shp = jax.ShapeDtypeStruct((m, HY_CH), F32)
    return pl.pallas_call(
        functools.partial(_hy_pre_kernel, tl=tl, blocks_per_seq=seq // tl),
        out_shape=(shp, shp),
        grid=(nblk,),
        in_specs=[cur(0), cur(1), cur(2), prv(0), prv(1), prv(2), nxt(0), nxt(1), nxt(2),
                  pl.BlockSpec((3, 3 * HY_CH), lambda i: (0, 0)),
                  pl.BlockSpec((1, 3 * HY_CH), lambda i: (0, 0))],
        out_specs=(out, out),
        compiler_params=_params(("parallel",)),
        name="hy_pre",
    )(z, z, z, z, z, z, z, z, z, conv_w, conv_b)


def _dft_rows_kernel(a_ref, x_ref, o_ref):
    o_ref[...] = jnp.dot(a_ref[...], x_ref[...].astype(BF16), preferred_element_type=F32)


def dft_rows(a, x, tn=2048):
    bsz, kdim, n = x.shape
    rows = a.shape[0]
    return pl.pallas_call(
        _dft_rows_kernel,
        out_shape=jax.ShapeDtypeStruct((bsz, rows, n), F32),
        grid=(bsz, n // tn),
        in_specs=[pl.BlockSpec((rows, kdim), lambda b, j: (0, 0)),
                  pl.BlockSpec((None, kdim, tn), lambda b, j: (b, 0, j))],
        out_specs=pl.BlockSpec((None, rows, tn), lambda b, j: (b, 0, j)),
        compiler_params=_params(("parallel", "parallel")),
        name="dft_rows",
    )(a, x)


def _cmul(ar, ai, br, bi):
    return ar * br - ai * bi, ar * bi + ai * br


def _dft_q(mat_ref, xr, xi):
    y = jnp.dot(mat_ref[...], jnp.concatenate([xr, xi], axis=0).astype(BF16),
                preferred_element_type=F32)
    return y[:DFT_Q], y[DFT_Q:]


def _hy_kspec_kernel(g_ref, tw_ref, fwd_ref, o_ref):
    tr, ti = tw_ref[0], tw_ref[1]
    fr, fi = _dft_q(fwd_ref, *_cmul(g_ref[0, 0], g_ref[0, 1], tr, ti))
    br, bi = _dft_q(fwd_ref, *_cmul(g_ref[1, 0], g_ref[1, 1], tr, ti))
    o_ref[0] = fr + br
    o_ref[1] = fi - bi


def hy_kspec(g, tw, fwd, tc=512):
    p = g.shape[2]
    return pl.pallas_call(
        _hy_kspec_kernel,
        out_shape=jax.ShapeDtypeStruct((2, p, DFT_Q, HY_CH), F32),
        grid=(p, HY_CH // tc),
        in_specs=[pl.BlockSpec((2, 2, None, DFT_Q, tc), lambda kp, c: (0, 0, kp, 0, c)),
                  pl.BlockSpec((None, 2, DFT_Q, 1), lambda kp, c: (kp, 0, 0, 0)),
                  pl.BlockSpec((2 * DFT_Q, 2 * DFT_Q), lambda kp, c: (0, 0))],
        out_specs=pl.BlockSpec((2, None, DFT_Q, tc), lambda kp, c: (0, kp, 0, c)),
        compiler_params=_params(("parallel", "parallel")),
        name="hy_kspec",
    )(g, tw, fwd)


def _hy_mid_kernel(g_ref, tw_ref, fwd_ref, inv_ref, ks_ref, o_ref):
    tr, ti = tw_ref[0], tw_ref[1]
    xr, xi = _dft_q(fwd_ref, *_cmul(g_ref[0], g_ref[1], tr, ti))
    yr, yi = _cmul(xr, xi, ks_ref[0], ks_ref[1])
    wr, wi = _dft_q(inv_ref, yr, yi)
    o_ref[0], o_ref[1] = _cmul(wr, wi, tr, -ti)


def hy_mid(g, tw, fwd, inv, kspec, tc=512):
    bsz, _, p = g.shape[:3]
    return pl.pallas_call(
        _hy_mid_kernel,
        out_shape=jax.ShapeDtypeStruct(g.shape, F32),
        grid=(bsz, p, HY_CH // tc),
        in_specs=[pl.BlockSpec((None, 2, None, DFT_Q, tc), lambda b, kp, c: (b, 0, kp, 0, c)),
                  pl.BlockSpec((None, 2, DFT_Q, 1), lambda b, kp, c: (kp, 0, 0, 0)),
                  pl.BlockSpec((2 * DFT_Q, 2 * DFT_Q), lambda b, kp, c: (0, 0)),
                  pl.BlockSpec((2 * DFT_Q, 2 * DFT_Q), lambda b, kp, c: (0, 0)),
                  pl.BlockSpec((2, None, DFT_Q, tc), lambda b, kp, c: (0, kp, 0, c))],
        out_specs=pl.BlockSpec((None, 2, None, DFT_Q, tc), lambda b, kp, c: (b, 0, kp, 0, c)),
        compiler_params=_params(("parallel", "parallel", "parallel")),
        name="hy_mid",
    )(g, tw, fwd, inv, kspec)


def _hy_post_kernel(a_ref, h_ref, u_ref, x0_ref, skip_ref, o_ref, *, inv_n):
    y = jnp.dot(a_ref[...], h_ref[...].astype(BF16), preferred_element_type=F32) * inv_n
    u = u_ref[...]
    o_ref[...] = x0_ref[...] * (y + u * skip_ref[...])


def hy_post(a, h, u, x0, skip_t, n_fft, tn=2048):
    bsz, rows2, n = h.shape
    ph = a.shape[0]
    return pl.pallas_call(
        functools.partial(_hy_post_kernel, inv_n=1.0 / n_fft),
        out_shape=jax.ShapeDtypeStruct((bsz, ph, n), F32),
        grid=(bsz, n // tn),
        in_specs=[pl.BlockSpec((ph, rows2), lambda b, j: (0, 0)),
                  pl.BlockSpec((None, rows2, tn), lambda b, j: (b, 0, j)),
                  pl.BlockSpec((None, ph, tn), lambda b, j: (b, 0, j)),
                  pl.BlockSpec((None, ph, tn), lambda b, j: (b, 0, j)),
                  pl.BlockSpec((1, tn), lambda b, j: (0, j))],
        out_specs=pl.BlockSpec((None, ph, tn), lambda b, j: (b, 0, j)),
        compiler_params=_params(("parallel", "parallel")),
        name="hy_post",
    )(a, h, u, x0, skip_t)


def _dft_constants(seq):
    n = 2 * seq
    q = DFT_Q
    p = n // q
    ph = p // 2
    fp = np.exp(-2j * np.pi * np.outer(np.arange(p), np.arange(p)) / p)
    fq = np.exp(-2j * np.pi * np.outer(np.arange(q), np.arange(q)) / q)
    tw = np.exp(-2j * np.pi * np.outer(np.arange(p), np.arange(q)) / n)
    a1 = np.concatenate([fp.real[:, :ph], fp.imag[:, :ph]], axis=0)
    fwd = np.block([[fq.real, -fq.imag], [fq.imag, fq.real]])
    inv = np.block([[fq.real, fq.imag], [-fq.imag, fq.real]])
    a3 = np.concatenate([fp.real[:ph, :], fp.imag[:ph, :]], axis=1)
    twa = np.stack([tw.real, tw.imag], axis=1)[..., None]
    cast = lambda a: jnp.asarray(a, F32).astype(BF16)
    return dict(n=n, p=p, ph=ph, a1=cast(a1), fwd=cast(fwd), inv=cast(inv), a3=cast(a3),
                tw=jnp.asarray(twa, F32))


def _hy_features(seq):
    t = jnp.linspace(0.0, 1.0, seq, dtype=F32)[:, None]
    w = 2.0 * math.pi * jnp.arange(seq, dtype=F32)[:, None] / seq
    f = jnp.linspace(1e-4, HY_BANDS - 1, HY_BANDS, dtype=F32)[None, :]
    z = jnp.concatenate([t, jnp.cos(f * w), -jnp.sin(f * w)], axis=-1)
    return jnp.pad(z, ((0, 0), (0, LANES - HY_EMB)))


def _hy_deltas():
    max_decay = math.log(1e-2) / 0.3
    min_decay = math.log(1e-2) / 1.5
    return jnp.abs(jnp.linspace(min_decay, max_decay, HY_CH, dtype=F32))[None, :]


def _pad2(a, rows, cols, value=0.0):
    return jnp.pad(a, ((0, rows - a.shape[0]), (0, cols - a.shape[1])), constant_values=value)


def hyena_spectrum(seq, consts, fw):
    h = hy_filter(_hy_features(seq), *fw, _hy_deltas(), tl=min(512, seq))
    g = dft_rows(consts["a1"], h.reshape(2, consts["ph"], DFT_Q * HY_CH))
    g = g.reshape(2, 2, consts["p"], DFT_Q, HY_CH)
    return hy_kspec(g, consts["tw"], consts["fwd"])


def hyena_conv(x0, u, batch, seq, consts, kspec, skip_t):
    ph, p = consts["ph"], consts["p"]
    uv = u.reshape(batch, ph, DFT_Q * HY_CH)
    g = dft_rows(consts["a1"], uv).reshape(batch, 2, p, DFT_Q, HY_CH)
    hm = hy_mid(g, consts["tw"], consts["fwd"], consts["inv"], kspec)
    y = hy_post(consts["a3"], hm.reshape(batch, 2 * p, DFT_Q * HY_CH), uv,
                x0.reshape(batch, ph, DFT_Q * HY_CH), skip_t, consts["n"])
    return y.reshape(batch * seq, HY_CH)


def _layer_params(l, norm_mix, w_in, hy_conv_w, hy_conv_b, hy_f_w1, hy_f_b1, hy_f_w2, hy_f_b2, hy_f_w3,
                  hy_f_b3, hy_f_w4, hy_f_b4, hy_f_freq, hy_skip, mla_q_a_norm, mla_w_q_b, mla_kv_a_norm,
                  mla_w_kv_b, mla_qn_nope, mla_qn_rope, mla_kn_nope, mla_kn_rope, dil_q_norm, dil_k_norm,
                  out_norm, w_out, norm_ffn, w_up, w_down):
    d = w_in.shape[1]
    wi = w_in[l]
    hy_cols = 3 * HY_CH
    o_cq = hy_cols
    o_ckv = o_cq + Q_LORA
    o_kr = o_ckv + KV_LORA
    o_dil = o_kr + ROPE_DIM
    w_cat = jnp.concatenate([
        wi[:, o_dil:], wi[:, o_cq:o_ckv], wi[:, :hy_cols], wi[:, o_ckv:o_kr],
        _spread_halves(wi[:, o_kr:o_dil])], axis=1).astype(BF16)
    wq = mla_w_q_b[l].reshape(Q_LORA, HEADS, QK_DIM)
    wq = jnp.concatenate([wq[..., :HEAD_DIM], _spread_halves(wq[..., HEAD_DIM:])], axis=-1)
    wq = wq.transpose(1, 0, 2).astype(BF16)
    wkv = mla_w_kv_b[l].reshape(KV_LORA, HEADS, 2 * HEAD_DIM).transpose(1, 0, 2).astype(BF16)
    hf = LANES
    filt = (_pad2(hy_f_w1[l], hf, hf).astype(BF16), _pad2(hy_f_b1[l][None], 1, hf),
            _pad2(hy_f_w2[l], hf, hf).astype(BF16), _pad2(hy_f_b2[l][None], 1, hf),
            _pad2(hy_f_w3[l], hf, hf).astype(BF16), _pad2(hy_f_b3[l][None], 1, hf),
            _pad2(hy_f_w4[l], hf, 2 * HY_CH).astype(BF16), hy_f_b4[l][None],
            _pad2(hy_f_freq[l][None], 1, hf, 1.0))
    return dict(
        norm_mix=norm_mix[l], w_in=w_cat,
        conv_w=hy_conv_w[l], conv_b=hy_conv_b[l][None], filt=filt,
        skip_t=jnp.tile(hy_skip[l], DFT_Q)[None],
        q_a_norm=mla_q_a_norm[l][None], w_q=wq,
        q_gain=jnp.concatenate([mla_qn_nope[l], _spread_halves(mla_qn_rope[l])])[None],
        kv_a_norm=mla_kv_a_norm[l][None], w_kv=wkv,
        kn_nope=mla_kn_nope[l][None], kn_rope=_spread_halves(mla_kn_rope[l])[None],
        dil_q=dil_q_norm[l][None], dil_k=dil_k_norm[l][None],
        out_norm=out_norm[l], w_out=w_out[l].astype(BF16),
        norm_ffn=norm_ffn[l], w_up=w_up[l].astype(BF16), w_down=w_down[l].astype(BF16))


def mixers(z, p, batch, seq, consts, tabs):
    mla_cos, mla_sin, dil_cos, dil_sin = tabs
    x0, u = hy_pre(z, p["conv_w"], p["conv_b"], seq)
    kspec = hyena_spectrum(seq, consts, p["filt"])
    y_hy = hyena_conv(x0, u, batch, seq, consts, kspec, p["skip_t"])
    q = mla_q(z, p["q_a_norm"], p["w_q"], p["q_gain"], mla_cos, mla_sin)
    k, v = mla_kv(z, p["kv_a_norm"], p["w_kv"], p["kn_nope"], p["kn_rope"], mla_cos, mla_sin)
    y_mla = mla_attention(q, k, v, batch, seq)
    qd, kd, vd = dil_prep(z, p["dil_q"], p["dil_k"], dil_cos, dil_sin)
    return (y_hy, y_mla) + tuple(dil_attention(qd, kd, vd, batch, seq))


def rope_tabs(batch, seq):
    pos = jnp.tile(jnp.arange(seq, dtype=jnp.int32), batch)
    c64, s64 = _rope_tables(pos, ROPE_DIM)
    c128, s128 = _rope_tables(pos, HEAD_DIM)
    return (_spread_halves(jnp.concatenate([c64, c64], -1)),
            _spread_halves(jnp.concatenate([-s64, s64], -1)),
            jnp.concatenate([c128, c128], -1), jnp.concatenate([-s128, s128], -1))


def _trunk(xin, layers):
    batch, seq, d = xin.shape
    x = xin.reshape(batch * seq, d)
    tabs = rope_tabs(batch, seq)
    consts = _dft_constants(seq)
    for p in layers:
        z = norm_matmul(x, p["norm_mix"], p["w_in"])
        ys = mixers(z, p, batch, seq, consts, tabs)
        x = out_proj(*ys, p["out_norm"], p["w_out"], x)
        x = ffn(x, p["norm_ffn"], p["w_up"], p["w_down"])
    return x.reshape(xin.shape)


def kernel(x_prompt, x_sample, norm_mix, w_in, hy_conv_w, hy_conv_b, hy_f_w1, hy_f_b1, hy_f_w2, hy_f_b2, hy_f_w3, hy_f_b3, hy_f_w4, hy_f_b4, hy_f_freq, hy_skip, mla_q_a_norm, mla_w_q_b, mla_kv_a_norm, mla_w_kv_b, mla_qn_nope, mla_qn_rope, mla_kn_nope, mla_kn_rope, dil_q_norm, dil_k_norm, out_norm, w_out, norm_ffn, w_up, w_down):
    weights = (norm_mix, w_in, hy_conv_w, hy_conv_b, hy_f_w1, hy_f_b1, hy_f_w2, hy_f_b2, hy_f_w3, hy_f_b3,
               hy_f_w4, hy_f_b4, hy_f_freq, hy_skip, mla_q_a_norm, mla_w_q_b, mla_kv_a_norm, mla_w_kv_b,
               mla_qn_nope, mla_qn_rope, mla_kn_nope, mla_kn_rope, dil_q_norm, dil_k_norm, out_norm, w_out,
               norm_ffn, w_up, w_down)
    layers = [_layer_params(l, *weights) for l in range(norm_mix.shape[0])]
    return tuple(_trunk(xin, layers) for xin in (x_prompt, x_sample))
```

```python
import functools
import math

import numpy as np
import jax
import jax.numpy as jnp
from jax import lax
from jax.experimental import pallas as pl
from jax.experimental.pallas import tpu as pltpu

F32 = jnp.float32
BF16 = jnp.bfloat16

EPS = 1e-6
NEG = -1e30
ROPE_THETA = 10000.0
LANES = 128
VMEM_LIMIT = 56 * 1024 * 1024

HY_CH = 1024
GROUP_W = 1536
HEADS = 12
HEAD_DIM = 128
ROPE_DIM = 64
QK_DIM = HEAD_DIM + ROPE_DIM
Q_LORA = 1536
KV_LORA = 512
DIL_PAIRS = ((128, 1), (512, 4), (2048, 16))
DIL_SLOTS = 4
HY_EMB = 33
HY_BANDS = 16
HY_FFN = 64

COL_DQ, COL_DK, COL_DV = 0, 1536, 3072
COL_CQ = 4608
COL_HY = 6144
COL_CKV = 9216
COL_KR = 9728
Z_USED = 9856
Z_COLS = 10240
DFT_Q = 128


def _params(sem, vmem=VMEM_LIMIT):
    return pltpu.CompilerParams(dimension_semantics=sem, vmem_limit_bytes=vmem)


def _rms_scale(x, width):
    return lax.rsqrt(jnp.sum(x * x, axis=-1, keepdims=True) * (1.0 / width) + EPS)


def _norm_matmul_kernel(x_ref, g_ref, w_ref, o_ref, h_ref):
    @pl.when(pl.program_id(1) == 0)
    def _():
        x = x_ref[...]
        h_ref[...] = (x * _rms_scale(x, x.shape[-1]) * g_ref[...]).astype(BF16)

    o_ref[...] = jnp.dot(h_ref[...], w_ref[...], preferred_element_type=F32)


def norm_matmul(x, g, w, tm=512, tn=1024):
    m, d = x.shape
    n = w.shape[1]
    return pl.pallas_call(
        _norm_matmul_kernel,
        out_shape=jax.ShapeDtypeStruct((m, n), F32),
        grid=(m // tm, n // tn),
        in_specs=[pl.BlockSpec((tm, d), lambda i, j: (i, 0)),
                  pl.BlockSpec((1, d), lambda i, j: (0, 0)),
                  pl.BlockSpec((d, tn), lambda i, j: (0, j))],
        out_specs=pl.BlockSpec((tm, tn), lambda i, j: (i, j)),
        scratch_shapes=[pltpu.VMEM((tm, d), BF16)],
        compiler_params=_params(("parallel", "arbitrary")),
        name="norm_matmul",
    )(x, g.reshape(1, d), w)


def _out_proj_kernel(hy_ref, mla_ref, d0_ref, d1_ref, d2_ref, g_ref, w_ref, x_ref, o_ref, h_ref):
    @pl.when(pl.program_id(1) == 0)
    def _():
        hy = hy_ref[...]
        h_ref[:, :HY_CH] = (hy * _rms_scale(hy, HY_CH) * g_ref[:, :HY_CH]).astype(BF16)
        ml = mla_ref[...]
        h_ref[:, HY_CH:HY_CH + GROUP_W] = (
            ml * _rms_scale(ml, GROUP_W) * g_ref[:, HY_CH:HY_CH + GROUP_W]).astype(BF16)
        d0, d1, d2 = d0_ref[...], d1_ref[...], d2_ref[...]
        ss = (jnp.sum(d0 * d0, axis=-1, keepdims=True) + jnp.sum(d1 * d1, axis=-1, keepdims=True)
              + jnp.sum(d2 * d2, axis=-1, keepdims=True))
        r = lax.rsqrt(ss * (1.0 / GROUP_W) + EPS)
        base = HY_CH + GROUP_W
        gw = GROUP_W // 3
        for k, dk in enumerate((d0, d1, d2)):
            lo = base + k * gw
            h_ref[:, lo:lo + gw] = (dk * r * g_ref[:, lo:lo + gw]).astype(BF16)

    o_ref[...] = x_ref[...] + jnp.dot(h_ref[...], w_ref[...], preferred_element_type=F32)


def out_proj(y_hy, y_mla, y_d0, y_d1, y_d2, g, w, x, layer, tm=512, tn=1024):
    m, d = x.shape
    mix = w.shape[1]
    gw = GROUP_W // 3
    row = lambda i, j: (i, 0)
    return pl.pallas_call(
        _out_proj_kernel,
        out_shape=jax.ShapeDtypeStruct((m, d), F32),
        grid=(m // tm, d // tn),
        in_specs=[pl.BlockSpec((tm, HY_CH), row),
                  pl.BlockSpec((tm, GROUP_W), row),
                  pl.BlockSpec((tm, gw), row),
                  pl.BlockSpec((tm, gw), row),
                  pl.BlockSpec((tm, gw), row),
                  pl.BlockSpec((1, mix), lambda i, j: (0, 0)),
                  pl.BlockSpec((None, mix, tn), lambda i, j: (layer, 0, j)),
                  pl.BlockSpec((tm, tn), lambda i, j: (i, j))],
        out_specs=pl.BlockSpec((tm, tn), lambda i, j: (i, j)),
        scratch_shapes=[pltpu.VMEM((tm, mix), BF16)],
        compiler_params=_params(("parallel", "arbitrary")),
        name="out_proj",
    )(y_hy, y_mla, y_d0, y_d1, y_d2, g.reshape(1, mix), w, x)


def _ffn_kernel(x_ref, g_ref, wu_ref, wd_ref, o_ref, h_ref):
    @pl.when(pl.program_id(1) == 0)
    def _():
        x = x_ref[...]
        h_ref[...] = (x * _rms_scale(x, x.shape[-1]) * g_ref[...]).astype(BF16)
        o_ref[...] = x

    a = jnp.maximum(jnp.dot(h_ref[...], wu_ref[...], preferred_element_type=F32), 0.0)
    o_ref[...] += jnp.dot((a * a).astype(BF16), wd_ref[...], preferred_element_type=F32)


def ffn(x, g, w_up, w_down, layer, tm=512, tf=512):
    m, d = x.shape
    f = w_up.shape[2]
    return pl.pallas_call(
        _ffn_kernel,
        out_shape=jax.ShapeDtypeStruct((m, d), F32),
        grid=(m // tm, f // tf),
        in_specs=[pl.BlockSpec((tm, d), lambda i, j: (i, 0), pipeline_mode=pl.Buffered(1)),
                  pl.BlockSpec((1, d), lambda i, j: (0, 0)),
                  pl.BlockSpec((None, d, tf), lambda i, j: (layer, 0, j)),
                  pl.BlockSpec((None, tf, d), lambda i, j: (layer, j, 0))],
        out_specs=pl.BlockSpec((tm, d), lambda i, j: (i, 0)),
        scratch_shapes=[pltpu.VMEM((tm, d), BF16)],
        compiler_params=_params(("parallel", "arbitrary")),
        name="ffn",
    )(x, g.reshape(1, d), w_up, w_down)


def _rope_tables(pos, dim):
    inv = 1.0 / (ROPE_THETA ** (jnp.arange(0, dim, 2, dtype=F32) / dim))
    ang = pos.astype(F32)[:, None] * inv[None, :]
    return jnp.cos(ang), jnp.sin(ang)


def _spread_halves(a):
    z = jnp.zeros(a.shape[:-1] + (ROPE_DIM // 2,), a.dtype)
    return jnp.concatenate([a[..., :ROPE_DIM // 2], z, a[..., ROPE_DIM // 2:], z], axis=-1)


def _rotate_half(x, cos, sin_signed):
    return x * cos + pltpu.roll(x, LANES // 2, axis=1) * sin_signed


def _mla_q_kernel(cq_ref, ga_ref, w_ref, gn_ref, cos_ref, sin_ref, q_ref, c_ref):
    @pl.when(pl.program_id(1) == 0)
    def _():
        c = cq_ref[...]
        c_ref[...] = (c * _rms_scale(c, Q_LORA) * ga_ref[...]).astype(BF16)

    q = jnp.dot(c_ref[...], w_ref[...], preferred_element_type=F32)
    scale = QK_DIM ** -0.5
    qn = q[:, :HEAD_DIM]
    qn = qn * _rms_scale(qn, HEAD_DIM) * gn_ref[:, :HEAD_DIM]
    qr = q[:, HEAD_DIM:]
    qr = qr * _rms_scale(qr, ROPE_DIM) * gn_ref[:, HEAD_DIM:]
    qr = _rotate_half(qr, cos_ref[...], sin_ref[...])
    q_ref[:, :HEAD_DIM] = (qn * scale).astype(BF16)
    q_ref[:, HEAD_DIM:] = (qr * scale).astype(BF16)


def mla_q(z, ga, w, gn, cos, sin, tm=512):
    m = z.shape[0]
    return pl.pallas_call(
        _mla_q_kernel,
        out_shape=jax.ShapeDtypeStruct((HEADS, m, 2 * LANES), BF16),
        grid=(m // tm, HEADS),
        in_specs=[pl.BlockSpec((tm, Q_LORA), lambda i, h: (i, COL_CQ // Q_LORA)),
                  pl.BlockSpec((1, Q_LORA), lambda i, h: (0, 0)),
                  pl.BlockSpec((None, Q_LORA, 2 * LANES), lambda i, h: (h, 0, 0)),
                  pl.BlockSpec((1, 2 * LANES), lambda i, h: (0, 0)),
                  pl.BlockSpec((tm, LANES), lambda i, h: (i, 0)),
                  pl.BlockSpec((tm, LANES), lambda i, h: (i, 0))],
        out_specs=pl.BlockSpec((None, tm, 2 * LANES), lambda i, h: (h, i, 0)),
        scratch_shapes=[pltpu.VMEM((tm, Q_LORA), BF16)],
        compiler_params=_params(("parallel", "arbitrary")),
        name="mla_q",
    )(z, ga, w, gn, cos, sin)


def _mla_kv_kernel(ckv_ref, kr_ref, ga_ref, w_ref, gk_ref, gr_ref, cos_ref, sin_ref,
                   k_ref, v_ref, c_ref, r_ref):
    @pl.when(pl.program_id(1) == 0)
    def _():
        c = ckv_ref[...]
        c_ref[...] = (c * _rms_scale(c, KV_LORA) * ga_ref[...]).astype(BF16)
        kr = kr_ref[...]
        kr = kr * _rms_scale(kr, ROPE_DIM) * gr_ref[...]
        r_ref[...] = _rotate_half(kr, cos_ref[...], sin_ref[...]).astype(BF16)

    kv = jnp.dot(c_ref[...], w_ref[...], preferred_element_type=F32)
    kn = kv[:, :HEAD_DIM]
    k_ref[:, :HEAD_DIM] = (kn * _rms_scale(kn, HEAD_DIM) * gk_ref[...]).astype(BF16)
    k_ref[:, HEAD_DIM:] = r_ref[...]
    v_ref[...] = kv[:, HEAD_DIM:].astype(BF16)


def mla_kv(z, ga, w, gk, gr, cos, sin, tm=512):
    m = z.shape[0]
    return pl.pallas_call(
        _mla_kv_kernel,
        out_shape=(jax.ShapeDtypeStruct((HEADS, m, 2 * LANES), BF16),
                   jax.ShapeDtypeStruct((HEADS, m, LANES), BF16)),
        grid=(m // tm, HEADS),
        in_specs=[pl.BlockSpec((tm, KV_LORA), lambda i, h: (i, COL_CKV // KV_LORA)),
                  pl.BlockSpec((tm, LANES), lambda i, h: (i, COL_KR // LANES)),
                  pl.BlockSpec((1, KV_LORA), lambda i, h: (0, 0)),
                  pl.BlockSpec((None, KV_LORA, 2 * LANES), lambda i, h: (h, 0, 0)),
                  pl.BlockSpec((1, LANES), lambda i, h: (0, 0)),
                  pl.BlockSpec((1, LANES), lambda i, h: (0, 0)),
                  pl.BlockSpec((tm, LANES), lambda i, h: (i, 0)),
                  pl.BlockSpec((tm, LANES), lambda i, h: (i, 0))],
        out_specs=(pl.BlockSpec((None, tm, 2 * LANES), lambda i, h: (h, i, 0)),
                   pl.BlockSpec((None, tm, LANES), lambda i, h: (h, i, 0))),
        scratch_shapes=[pltpu.VMEM((tm, KV_LORA), BF16), pltpu.VMEM((tm, LANES), BF16)],
        compiler_params=_params(("parallel", "arbitrary")),
        name="mla_kv",
    )(z, z, ga, w, gk, gr, cos, sin)


def _flash_kernel(q_ref, k_ref, v_ref, o_ref, m_ref, l_ref, acc_ref, *, nk):
    j = pl.program_id(3)

    @pl.when(j == 0)
    def _():
        m_ref[...] = jnp.full(m_ref.shape, -jnp.inf, F32)
        l_ref[...] = jnp.zeros(l_ref.shape, F32)
        acc_ref[...] = jnp.zeros(acc_ref.shape, F32)

    s = lax.dot_general(q_ref[...], k_ref[...], (((1,), (1,)), ((), ())),
                        preferred_element_type=F32)
    m_prev = m_ref[...]
    m_new = jnp.maximum(m_prev, jnp.max(s, axis=-1, keepdims=True))
    alpha = jnp.exp(m_prev - m_new)
    p = jnp.exp(s - m_new)
    l_ref[...] = alpha * l_ref[...] + jnp.sum(p, axis=-1, keepdims=True)
    acc_ref[...] = alpha * acc_ref[...] + jnp.dot(p.astype(BF16), v_ref[...],
                                                  preferred_element_type=F32)
    m_ref[...] = m_new

    @pl.when(j == nk - 1)
    def _():
        o_ref[...] = acc_ref[...] / l_ref[...]


def mla_attention(q, k, v, batch, seq, tq=512, tk=2048):
    tq, tk = min(tq, seq), min(tk, seq)
    nq, nk = seq // tq, seq // tk
    return pl.pallas_call(
        functools.partial(_flash_kernel, nk=nk),
        out_shape=jax.ShapeDtypeStruct((batch * seq, GROUP_W), F32),
        grid=(batch, HEADS, nq, nk),
        in_specs=[pl.BlockSpec((None, tq, 2 * LANES), lambda b, h, i, j: (h, b * nq + i, 0)),
                  pl.BlockSpec((None, tk, 2 * LANES), lambda b, h, i, j: (h, b * nk + j, 0)),
                  pl.BlockSpec((None, tk, LANES), lambda b, h, i, j: (h, b * nk + j, 0))],
        out_specs=pl.BlockSpec((tq, LANES), lambda b, h, i, j: (b * nq + i, h)),
        scratch_shapes=[pltpu.VMEM((tq, 1), F32), pltpu.VMEM((tq, 1), F32),
                        pltpu.VMEM((tq, LANES), F32)],
        compiler_params=_params(("parallel", "parallel", "parallel", "arbitrary")),
        name="mla_attention",
    )(q, k, v)


def _dil_prep_kernel(q_ref, k_ref, v_ref, gq_ref, gk_ref, cos_ref, sin_ref, qo_ref, ko_ref, vo_ref):
    cos, sin = cos_ref[...], sin_ref[...]
    scale = HEAD_DIM ** -0.5
    for h in range(HEADS):
        sl = slice(h * HEAD_DIM, (h + 1) * HEAD_DIM)
        q = q_ref[:, sl]
        q = _rotate_half(q * _rms_scale(q, HEAD_DIM) * gq_ref[...], cos, sin)
        qo_ref[:, sl] = (q * scale).astype(BF16)
        k = k_ref[:, sl]
        k = _rotate_half(k * _rms_scale(k, HEAD_DIM) * gk_ref[...], cos, sin)
        ko_ref[:, sl] = k.astype(BF16)
    vo_ref[...] = v_ref[...].astype(BF16)


def dil_prep(z, gq, gk, cos, sin, tm=512):
    m = z.shape[0]
    sec = lambda c: pl.BlockSpec((tm, GROUP_W), lambda i: (i, c // GROUP_W))
    vec = pl.BlockSpec((1, LANES), lambda i: (0, 0))
    tab = pl.BlockSpec((tm, LANES), lambda i: (i, 0))
    out = pl.BlockSpec((tm, GROUP_W), lambda i: (i, 0))
    shp = jax.ShapeDtypeStruct((m, GROUP_W), BF16)
    return pl.pallas_call(
        _dil_prep_kernel,
        out_shape=(shp, shp, shp),
        grid=(m // tm,),
        in_specs=[sec(COL_DQ), sec(COL_DK), sec(COL_DV), vec, vec, tab, tab],
        out_specs=(out, out, out),
        compiler_params=_params(("parallel",)),
        name="dil_prep",
    )(z, z, z, gq, gk, cos, sin)


def _dil_attn_kernel(*refs, seq, tq):
    q_refs, k_refs, v_refs, o_refs = refs[0:3], refs[3:6], refs[6:9], refs[9:12]
    i = pl.program_id(2)
    outs, lses = [], []
    for g, (window, dil) in enumerate(DIL_PAIRS):
        reach = window // 2
        win = min(seq, tq + 2 * reach)
        start = jnp.clip(i * tq - reach, 0, seq - win)
        start = pl.multiple_of(start, 64)
        k = k_refs[g][pl.ds(start, win), :]
        v = v_refs[g][pl.ds(start, win), :]
        s = lax.dot_general(q_refs[g][...], k, (((1,), (1,)), ((), ())),
                            preferred_element_type=F32)
        qpos = i * tq + lax.broadcasted_iota(jnp.int32, (tq, win), 0)
        kpos = start + lax.broadcasted_iota(jnp.int32, (tq, win), 1)
        diff = kpos - qpos
        valid = (jnp.abs(diff) <= reach) & ((diff & (dil - 1)) == 0)
        s = jnp.where(valid, s, NEG)
        m = jnp.max(s, axis=-1, keepdims=True)
        p = jnp.exp(s - m)
        l = jnp.sum(p, axis=-1, keepdims=True)
        o = jnp.dot(p.astype(BF16), v, preferred_element_type=F32) / l
        outs.append(o)
        lses.append(m + jnp.log(l))
    top = jnp.maximum(jnp.maximum(lses[0], lses[1]), lses[2])
    es = [jnp.exp(t - top) for t in lses]
    den = es[0] + es[1] + es[2]
    for g in range(3):
        o_refs[g][...] = outs[g] * (es[g] / den)


def dil_attention(qd, kd, vd, batch, seq, tq=256):
    tq = min(tq, seq)
    nq = seq // tq
    qs = [pl.BlockSpec((tq, HEAD_DIM), functools.partial(
        lambda b, s, i, g: (b * nq + i, g * DIL_SLOTS + s), g=g)) for g in range(3)]
    ks = [pl.BlockSpec((seq, HEAD_DIM), functools.partial(
        lambda b, s, i, g: (b, g * DIL_SLOTS + s), g=g)) for g in range(3)]
    os_ = [pl.BlockSpec((tq, HEAD_DIM), lambda b, s, i: (b * nq + i, s)) for _ in range(3)]
    shp = jax.ShapeDtypeStruct((batch * seq, DIL_SLOTS * HEAD_DIM), F32)
    return pl.pallas_call(
        functools.partial(_dil_attn_kernel, seq=seq, tq=tq),
        out_shape=(shp, shp, shp),
        grid=(batch, DIL_SLOTS, nq),
        in_specs=qs + ks + ks,
        out_specs=tuple(os_),
        compiler_params=_params(("parallel", "parallel", "arbitrary")),
        name="dil_attention",
    )(qd, qd, qd, kd, kd, kd, vd, vd, vd)


def _hy_filter_kernel(z_ref, w1_ref, b1_ref, w2_ref, b2_ref, w3_ref, b3_ref, w4_ref, b4_ref,
                      fq_ref, dl_ref, o_ref, *, tl):
    fq = fq_ref[...]
    h = z_ref[...]
    for w_ref, b_ref in ((w1_ref, b1_ref), (w2_ref, b2_ref), (w3_ref, b3_ref)):
        h = jnp.sin(fq * (jnp.dot(h.astype(BF16), w_ref[...], preferred_element_type=F32) + b_ref[...]))
    h = jnp.dot(h.astype(BF16), w4_ref[...], preferred_element_type=F32) + b4_ref[...]
    t = z_ref[:, 0:1]
    decay = jnp.exp(-t * dl_ref[...])
    o_ref[0] = h[:, :HY_CH] * decay
    n = pl.program_id(0) * tl + lax.broadcasted_iota(jnp.int32, (tl, 1), 0)
    o_ref[1] = jnp.where(n == 0, 0.0, h[:, HY_CH:] * decay)


def hy_filter(zfeat, w1, b1, w2, b2, w3, b3, w4, b4, fq, deltas, tl=512):
    seq = zfeat.shape[0]
    full = lambda a: pl.BlockSpec(a.shape, lambda i: (0,) * a.ndim)
    args = (w1, b1, w2, b2, w3, b3, w4, b4, fq, deltas)
    return pl.pallas_call(
        functools.partial(_hy_filter_kernel, tl=tl),
        out_shape=jax.ShapeDtypeStruct((2, seq, HY_CH), F32),
        grid=(seq // tl,),
        in_specs=[pl.BlockSpec((tl, LANES), lambda i: (i, 0))] + [full(a) for a in args],
        out_specs=pl.BlockSpec((2, tl, HY_CH), lambda i: (0, i, 0)),
        compiler_params=_params(("parallel",)),
        name="hy_filter",
    )(zfeat, *args)


def _hy_pre_kernel(x0_ref, x1_ref, v_ref, p0_ref, p1_ref, pv_ref, n0_ref, n1_ref, nv_ref,
                   w_ref, b_ref, x0o_ref, u_ref, *, tl, blocks_per_seq):
    i = pl.program_id(0)
    first = (i % blocks_per_seq) == 0
    last = (i % blocks_per_seq) == blocks_per_seq - 1
    row = lax.broadcasted_iota(jnp.int32, (tl, 1), 0)

    def conv(c_ref, p_ref, n_ref, part):
        c = c_ref[...]
        sl = slice(part * HY_CH, (part + 1) * HY_CH)
        prev_row = jnp.where(first, 0.0, p_ref[7:8, :])
        next_row = jnp.where(last, 0.0, n_ref[0:1, :])
        down = jnp.where(row == 0, prev_row, pltpu.roll(c, 1, axis=0))
        up = jnp.where(row == tl - 1, next_row, pltpu.roll(c, tl - 1, axis=0))
        return down * w_ref[0:1, sl] + c * w_ref[1:2, sl] + up * w_ref[2:3, sl] + b_ref[:, sl]

    x0o_ref[...] = conv(x0_ref, p0_ref, n0_ref, 0)
    u_ref[...] = conv(x1_ref, p1_ref, n1_ref, 1) * conv(v_ref, pv_ref, nv_ref, 2)


def hy_pre(z, conv_w, conv_b, seq, tl=512):
    m = z.shape[0]
    nblk = m // tl
    r8 = tl // 8
    c0 = COL_HY // HY_CH
    cur = lambda p: pl.BlockSpec((tl, HY_CH), lambda i: (i, c0 + p))
    prv = lambda p: pl.BlockSpec((8, HY_CH), lambda i: (jnp.maximum(i * r8 - 1, 0), c0 + p))
    nxt = lambda p: pl.BlockSpec((8, HY_CH), lambda i: (jnp.minimum((i + 1) * r8, m // 8 - 1), c0 + p))
    out = pl.BlockSpec((tl, HY_CH), lambda i: (i, 0))
    shp = jax.ShapeDtypeStruct((m, HY_CH), F32)
    return pl.pallas_call(
        functools.partial(_hy_pre_kernel, tl=tl, blocks_per_seq=seq // tl),
        out_shape=(shp, shp),
        grid=(nblk,),
        in_specs=[cur(0), cur(1), cur(2), prv(0), prv(1), prv(2), nxt(0), nxt(1), nxt(2),
                  pl.BlockSpec((3, 3 * HY_CH), lambda i: (0, 0)),
                  pl.BlockSpec((1, 3 * HY_CH), lambda i: (0, 0))],
        out_specs=(out, out),
        compiler_params=_params(("parallel",)),
        name="hy_pre",
    )(z, z, z, z, z, z, z, z, z, conv_w, conv_b)


def _dft_rows_kernel(a_ref, x_ref, o_ref):
    o_ref[...] = jnp.dot(a_ref[...], x_ref[...].astype(BF16), preferred_element_type=F32)


def dft_rows(a, x, tn=2048):
    bsz, kdim, n = x.shape
    rows = a.shape[0]
    return pl.pallas_call(
        _dft_rows_kernel,
        out_shape=jax.ShapeDtypeStruct((bsz, rows, n), F32),
        grid=(bsz, n // tn),
        in_specs=[pl.BlockSpec((rows, kdim), lambda b, j: (0, 0)),
                  pl.BlockSpec((None, kdim, tn), lambda b, j: (b, 0, j))],
        out_specs=pl.BlockSpec((None, rows, tn), lambda b, j: (b, 0, j)),
        compiler_params=_params(("parallel", "parallel")),
        name="dft_rows",
    )(a, x)


def _cmul(ar, ai, br, bi):
    return ar * br - ai * bi, ar * bi + ai * br


def _dft_q(mat_ref, xr, xi):
    y = jnp.dot(mat_ref[...], jnp.concatenate([xr, xi], axis=0).astype(BF16),
                preferred_element_type=F32)
    return y[:DFT_Q], y[DFT_Q:]


def _hy_kspec_kernel(g_ref, tw_ref, fwd_ref, o_ref):
    tr, ti = tw_ref[0], tw_ref[1]
    fr, fi = _dft_q(fwd_ref, *_cmul(g_ref[0, 0], g_ref[0, 1], tr, ti))
    br, bi = _dft_q(fwd_ref, *_cmul(g_ref[1, 0], g_ref[1, 1], tr, ti))
    o_ref[0] = fr + br
    o_ref[1] = fi - bi


def hy_kspec(g, tw, fwd, tc=512):
    p = g.shape[2]
    return pl.pallas_call(
        _hy_kspec_kernel,
        out_shape=jax.ShapeDtypeStruct((2, p, DFT_Q, HY_CH), F32),
        grid=(p, HY_CH // tc),
        in_specs=[pl.BlockSpec((2, 2, None, DFT_Q, tc), lambda kp, c: (0, 0, kp, 0, c)),
                  pl.BlockSpec((None, 2, DFT_Q, 1), lambda kp, c: (kp, 0, 0, 0)),
                  pl.BlockSpec((2 * DFT_Q, 2 * DFT_Q), lambda kp, c: (0, 0))],
        out_specs=pl.BlockSpec((2, None, DFT_Q, tc), lambda kp, c: (0, kp, 0, c)),
        compiler_params=_params(("parallel", "parallel")),
        name="hy_kspec",
    )(g, tw, fwd)


def _hy_mid_kernel(g_ref, tw_ref, fwd_ref, inv_ref, ks_ref, o_ref):
    tr, ti = tw_ref[0], tw_ref[1]
    xr, xi = _dft_q(fwd_ref, *_cmul(g_ref[0], g_ref[1], tr, ti))
    yr, yi = _cmul(xr, xi, ks_ref[0], ks_ref[1])
    wr, wi = _dft_q(inv_ref, yr, yi)
    o_ref[0], o_ref[1] = _cmul(wr, wi, tr, -ti)


def hy_mid(g, tw, fwd, inv, kspec, tc=512):
    bsz, _, p = g.shape[:3]
    return pl.pallas_call(
        _hy_mid_kernel,
        out_shape=jax.ShapeDtypeStruct(g.shape, F32),
        grid=(bsz, p, HY_CH // tc),
        in_specs=[pl.BlockSpec((None, 2, None, DFT_Q, tc), lambda b, kp, c: (b, 0, kp, 0, c)),
                  pl.BlockSpec((None, 2, DFT_Q, 1), lambda b, kp, c: (kp, 0, 0, 0)),
                  pl.BlockSpec((2 * DFT_Q, 2 * DFT_Q), lambda b, kp, c: (0, 0)),
                  pl.BlockSpec((2 * DFT_Q, 2 * DFT_Q), lambda b, kp, c: (0, 0)),
                  pl.BlockSpec((2, None, DFT_Q, tc), lambda b, kp, c: (0, kp, 0, c))],
        out_specs=pl.BlockSpec((None, 2, None, DFT_Q, tc), lambda b, kp, c: (b, 0, kp, 0, c)),
        compiler_params=_params(("parallel", "parallel", "parallel")),
        name="hy_mid",
    )(g, tw, fwd, inv, kspec)


def _hy_post_kernel(a_ref, h_ref, u_ref, x0_ref, skip_ref, o_ref, *, inv_n):
    y = jnp.dot(a_ref[...], h_ref[...].astype(BF16), preferred_element_type=F32) * inv_n
    u = u_ref[...]
    o_ref[...] = x0_ref[...] * (y + u * skip_ref[...])


def hy_post(a, h, u, x0, skip_t, n_fft, tn=2048):
    bsz, rows2, n = h.shape
    ph = a.shape[0]
    return pl.pallas_call(
        functools.partial(_hy_post_kernel, inv_n=1.0 / n_fft),
        out_shape=jax.ShapeDtypeStruct((bsz, ph, n), F32),
        grid=(bsz, n // tn),
        in_specs=[pl.BlockSpec((ph, rows2), lambda b, j: (0, 0)),
                  pl.BlockSpec((None, rows2, tn), lambda b, j: (b, 0, j)),
                  pl.BlockSpec((None, ph, tn), lambda b, j: (b, 0, j)),
                  pl.BlockSpec((None, ph, tn), lambda b, j: (b, 0, j)),
                  pl.BlockSpec((1, tn), lambda b, j: (0, j))],
        out_specs=pl.BlockSpec((None, ph, tn), lambda b, j: (b, 0, j)),
        compiler_params=_params(("parallel", "parallel")),
        name="hy_post",
    )(a, h, u, x0, skip_t)


def _dft_constants(seq):
    n = 2 * seq
    q = DFT_Q
    p = n // q
    ph = p // 2
    fp = np.exp(-2j * np.pi * np.outer(np.arange(p), np.arange(p)) / p)
    fq = np.exp(-2j * np.pi * np.outer(np.arange(q), np.arange(q)) / q)
    tw = np.exp(-2j * np.pi * np.outer(np.arange(p), np.arange(q)) / n)
    a1 = np.concatenate([fp.real[:, :ph], fp.imag[:, :ph]], axis=0)
    fwd = np.block([[fq.real, -fq.imag], [fq.imag, fq.real]])
    inv = np.block([[fq.real, fq.imag], [-fq.imag, fq.real]])
    a3 = np.concatenate([fp.real[:ph, :], fp.imag[:ph, :]], axis=1)
    twa = np.stack([tw.real, tw.imag], axis=1)[..., None]
    cast = lambda a: jnp.asarray(a, F32).astype(BF16)
    return dict(n=n, p=p, ph=ph, a1=cast(a1), fwd=cast(fwd), inv=cast(inv), a3=cast(a3),
                tw=jnp.asarray(twa, F32))


def _hy_features(seq):
    t = jnp.linspace(0.0, 1.0, seq, dtype=F32)[:, None]
    w = 2.0 * math.pi * jnp.arange(seq, dtype=F32)[:, None] / seq
    f = jnp.linspace(1e-4, HY_BANDS - 1, HY_BANDS, dtype=F32)[None, :]
    z = jnp.concatenate([t, jnp.cos(f * w), -jnp.sin(f * w)], axis=-1)
    return jnp.pad(z, ((0, 0), (0, LANES - HY_EMB)))


def _hy_deltas():
    max_decay = math.log(1e-2) / 0.3
    min_decay = math.log(1e-2) / 1.5
    return jnp.abs(jnp.linspace(min_decay, max_decay, HY_CH, dtype=F32))[None, :]


def _pad2(a, rows, cols, value=0.0):
    return jnp.pad(a, ((0, rows - a.shape[0]), (0, cols - a.shape[1])), constant_values=value)


def hyena_spectrum(seq, consts, fw):
    h = hy_filter(_hy_features(seq), *fw, _hy_deltas(), tl=min(512, seq))
    g = dft_rows(consts["a1"], h.reshape(2, consts["ph"], DFT_Q * HY_CH))
    g = g.reshape(2, 2, consts["p"], DFT_Q, HY_CH)
    return hy_kspec(g, consts["tw"], consts["fwd"])


def hyena_conv(x0, u, batch, seq, consts, kspec, skip_t):
    ph, p = consts["ph"], consts["p"]
    uv = u.reshape(batch, ph, DFT_Q * HY_CH)
    g = dft_rows(consts["a1"], uv).reshape(batch, 2, p, DFT_Q, HY_CH)
    hm = hy_mid(g, consts["tw"], consts["fwd"], consts["inv"], kspec)
    y = hy_post(consts["a3"], hm.reshape(batch, 2 * p, DFT_Q * HY_CH), uv,
                x0.reshape(batch, ph, DFT_Q * HY_CH), skip_t, consts["n"])
    return y.reshape(batch * seq, HY_CH)


def _layer_params(l, norm_mix, w_in, hy_conv_w, hy_conv_b, hy_f_w1, hy_f_b1, hy_f_w2, hy_f_b2, hy_f_w3,
                  hy_f_b3, hy_f_w4, hy_f_b4, hy_f_freq, hy_skip, mla_q_a_norm, mla_w_q_b, mla_kv_a_norm,
                  mla_w_kv_b, mla_qn_nope, mla_qn_rope, mla_kn_nope, mla_kn_rope, dil_q_norm, dil_k_norm,
                  out_norm, w_out, norm_ffn, w_up, w_down):
    d = w_in.shape[1]
    wi = w_in[l]
    hy_cols = 3 * HY_CH
    o_cq = hy_cols
    o_ckv = o_cq + Q_LORA
    o_kr = o_ckv + KV_LORA
    o_dil = o_kr + ROPE_DIM
    w_cat = jnp.concatenate([
        wi[:, o_dil:], wi[:, o_cq:o_ckv], wi[:, :hy_cols], wi[:, o_ckv:o_kr],
        _spread_halves(wi[:, o_kr:o_dil]), jnp.zeros((d, Z_COLS - Z_USED), wi.dtype)], axis=1).astype(BF16)
    wq = mla_w_q_b[l].reshape(Q_LORA, HEADS, QK_DIM)
    wq = jnp.concatenate([wq[..., :HEAD_DIM], _spread_halves(wq[..., HEAD_DIM:])], axis=-1)
    wq = wq.transpose(1, 0, 2).astype(BF16)
    wkv = mla_w_kv_b[l].reshape(KV_LORA, HEADS, 2 * HEAD_DIM).transpose(1, 0, 2).astype(BF16)
    hf = LANES
    filt = (_pad2(hy_f_w1[l], hf, hf).astype(BF16), _pad2(hy_f_b1[l][None], 1, hf),
            _pad2(hy_f_w2[l], hf, hf).astype(BF16), _pad2(hy_f_b2[l][None], 1, hf),
            _pad2(hy_f_w3[l], hf, hf).astype(BF16), _pad2(hy_f_b3[l][None], 1, hf),
            _pad2(hy_f_w4[l], hf, 2 * HY_CH).astype(BF16), hy_f_b4[l][None],
            _pad2(hy_f_freq[l][None], 1, hf, 1.0))
    return dict(
        norm_mix=norm_mix[l], w_in=w_cat,
        conv_w=hy_conv_w[l], conv_b=hy_conv_b[l][None], filt=filt,
        skip_t=jnp.tile(hy_skip[l], DFT_Q)[None],
        q_a_norm=mla_q_a_norm[l][None], w_q=wq,
        q_gain=jnp.concatenate([mla_qn_nope[l], _spread_halves(mla_qn_rope[l])])[None],
        kv_a_norm=mla_kv_a_norm[l][None], w_kv=wkv,
        kn_nope=mla_kn_nope[l][None], kn_rope=_spread_halves(mla_kn_rope[l])[None],
        dil_q=dil_q_norm[l][None], dil_k=dil_k_norm[l][None],
        out_norm=out_norm[l], norm_ffn=norm_ffn[l], layer=l)


def mixers(z, p, batch, seq, consts, tabs):
    mla_cos, mla_sin, dil_cos, dil_sin = tabs
    x0, u = hy_pre(z, p["conv_w"], p["conv_b"], seq)
    kspec = hyena_spectrum(seq, consts, p["filt"])
    y_hy = hyena_conv(x0, u, batch, seq, consts, kspec, p["skip_t"])
    q = mla_q(z, p["q_a_norm"], p["w_q"], p["q_gain"], mla_cos, mla_sin)
    k, v = mla_kv(z, p["kv_a_norm"], p["w_kv"], p["kn_nope"], p["kn_rope"], mla_cos, mla_sin)
    y_mla = mla_attention(q, k, v, batch, seq)
    qd, kd, vd = dil_prep(z, p["dil_q"], p["dil_k"], dil_cos, dil_sin)
    return (y_hy, y_mla) + tuple(dil_attention(qd, kd, vd, batch, seq))


def rope_tabs(batch, seq):
    pos = jnp.tile(jnp.arange(seq, dtype=jnp.int32), batch)
    c64, s64 = _rope_tables(pos, ROPE_DIM)
    c128, s128 = _rope_tables(pos, HEAD_DIM)
    return (_spread_halves(jnp.concatenate([c64, c64], -1)),
            _spread_halves(jnp.concatenate([-s64, s64], -1)),
            jnp.concatenate([c128, c128], -1), jnp.concatenate([-s128, s128], -1))


def _trunk(xin, layers, shared):
    batch, seq, d = xin.shape
    x = xin.reshape(batch * seq, d)
    tabs = rope_tabs(batch, seq)
    consts = _dft_constants(seq)
    for p in layers:
        z = norm_matmul(x, p["norm_mix"], p["w_in"])
        ys = mixers(z, p, batch, seq, consts, tabs)
        x = out_proj(*ys, p["out_norm"], shared["w_out"], x, p["layer"])
        x = ffn(x, p["norm_ffn"], shared["w_up"], shared["w_down"], p["layer"])
    return x.reshape(xin.shape)


def kernel(x_prompt, x_sample, norm_mix, w_in, hy_conv_w, hy_conv_b, hy_f_w1, hy_f_b1, hy_f_w2, hy_f_b2, hy_f_w3, hy_f_b3, hy_f_w4, hy_f_b4, hy_f_freq, hy_skip, mla_q_a_norm, mla_w_q_b, mla_kv_a_norm, mla_w_kv_b, mla_qn_nope, mla_qn_rope, mla_kn_nope, mla_kn_rope, dil_q_norm, dil_k_norm, out_norm, w_out, norm_ffn, w_up, w_down):
    weights = (norm_mix, w_in, hy_conv_w, hy_conv_b, hy_f_w1, hy_f_b1, hy_f_w2, hy_f_b2, hy_f_w3, hy_f_b3,
               hy_f_w4, hy_f_b4, hy_f_freq, hy_skip, mla_q_a_norm, mla_w_q_b, mla_kv_a_norm, mla_w_kv_b,
               mla_qn_nope, mla_qn_rope, mla_kn_nope, mla_kn_rope, dil_q_norm, dil_k_norm, out_norm, w_out,
               norm_ffn, w_up, w_down)
    layers = [_layer_params(l, *weights) for l in range(norm_mix.shape[0])]
    shared = dict(w_out=w_out.astype(BF16), w_up=w_up.astype(BF16), w_down=w_down.astype(BF16))
    return tuple(_trunk(xin, layers, shared) for xin in (x_prompt, x_sample))
```

```python
import functools
import math

import numpy as np
import jax
import jax.numpy as jnp
from jax import lax
from jax.experimental import pallas as pl
from jax.experimental.pallas import tpu as pltpu

F32 = jnp.float32
BF16 = jnp.bfloat16

EPS = 1e-6
NEG = -1e30
ROPE_THETA = 10000.0
LANES = 128
VMEM_LIMIT = 56 * 1024 * 1024

HY_CH = 1024
GROUP_W = 1536
HEADS = 12
HEAD_DIM = 128
ROPE_DIM = 64
QK_DIM = HEAD_DIM + ROPE_DIM
Q_LORA = 1536
KV_LORA = 512
DIL_PAIRS = ((128, 1), (512, 4), (2048, 16))
DIL_SLOTS = 4
HY_EMB = 33
HY_BANDS = 16
HY_FFN = 64
HEAD_GROUP = 4

COL_DQ, COL_DK, COL_DV = 0, 1536, 3072
COL_CQ = 4608
COL_HY = 6144
COL_CKV = 9216
COL_KR = 9728
Z_USED = 9856
Z_COLS = 10240
DFT_Q = 128


def _params(sem, vmem=VMEM_LIMIT):
    return pltpu.CompilerParams(dimension_semantics=sem, vmem_limit_bytes=vmem)


def _rms_scale(x, width):
    return lax.rsqrt(jnp.sum(x * x, axis=-1, keepdims=True) * (1.0 / width) + EPS)


NORM_ROWS = 256


def _norm_matmul_kernel(x_ref, g_ref, w_ref, o_ref, h_ref):
    @pl.when(pl.program_id(1) == 0)
    def _():
        for r in range(0, x_ref.shape[0], NORM_ROWS):
            x = x_ref[r:r + NORM_ROWS, :]
            h_ref[r:r + NORM_ROWS, :] = (x * _rms_scale(x, x.shape[-1]) * g_ref[...]).astype(BF16)

    o_ref[...] = jnp.dot(h_ref[...], w_ref[...], preferred_element_type=F32)


def norm_matmul(x, g, w, tm=1024, tn=512):
    m, d = x.shape
    n = w.shape[1]
    return pl.pallas_call(
        _norm_matmul_kernel,
        out_shape=jax.ShapeDtypeStruct((m, n), F32),
        grid=(m // tm, n // tn),
        in_specs=[pl.BlockSpec((tm, d), lambda i, j: (i, 0), pipeline_mode=pl.Buffered(1)),
                  pl.BlockSpec((1, d), lambda i, j: (0, 0)),
                  pl.BlockSpec((d, tn), lambda i, j: (0, j))],
        out_specs=pl.BlockSpec((tm, tn), lambda i, j: (i, j)),
        scratch_shapes=[pltpu.VMEM((tm, d), BF16)],
        compiler_params=_params(("parallel", "arbitrary")),
        name="norm_matmul",
    )(x, g.reshape(1, d), w)


def _out_proj_kernel(hy_ref, mla_ref, d0_ref, d1_ref, d2_ref, g_ref, w_ref, x_ref, o_ref, h_ref):
    @pl.when(pl.program_id(1) == 0)
    def _():
        base = HY_CH + GROUP_W
        gw = GROUP_W // 3
        for r0 in range(0, h_ref.shape[0], NORM_ROWS):
            rs = slice(r0, r0 + NORM_ROWS)
            hy = hy_ref[rs, :]
            h_ref[rs, :HY_CH] = (hy * _rms_scale(hy, HY_CH) * g_ref[:, :HY_CH]).astype(BF16)
            ml = mla_ref[rs, :]
            h_ref[rs, HY_CH:base] = (ml * _rms_scale(ml, GROUP_W) * g_ref[:, HY_CH:base]).astype(BF16)
            d0, d1, d2 = d0_ref[rs, :], d1_ref[rs, :], d2_ref[rs, :]
            ss = (jnp.sum(d0 * d0, axis=-1, keepdims=True) + jnp.sum(d1 * d1, axis=-1, keepdims=True)
                  + jnp.sum(d2 * d2, axis=-1, keepdims=True))
            r = lax.rsqrt(ss * (1.0 / GROUP_W) + EPS)
            for k, dk in enumerate((d0, d1, d2)):
                lo = base + k * gw
                h_ref[rs, lo:lo + gw] = (dk * r * g_ref[:, lo:lo + gw]).astype(BF16)

    o_ref[...] = x_ref[...] + jnp.dot(h_ref[...], w_ref[...], preferred_element_type=F32)


def out_proj(y_hy, y_mla, y_d0, y_d1, y_d2, g, w, x, layer, tm=1024, tn=512):
    m, d = x.shape
    mix = w.shape[1]
    gw = GROUP_W // 3
    row = lambda i, j: (i, 0)
    once = pl.Buffered(1)
    return pl.pallas_call(
        _out_proj_kernel,
        out_shape=jax.ShapeDtypeStruct((m, d), F32),
        grid=(m // tm, d // tn),
        in_specs=[pl.BlockSpec((tm, HY_CH), row, pipeline_mode=once),
                  pl.BlockSpec((tm, GROUP_W), row, pipeline_mode=once),
                  pl.BlockSpec((tm, gw), row, pipeline_mode=once),
                  pl.BlockSpec((tm, gw), row, pipeline_mode=once),
                  pl.BlockSpec((tm, gw), row, pipeline_mode=once),
                  pl.BlockSpec((1, mix), lambda i, j: (0, 0)),
                  pl.BlockSpec((None, mix, tn), lambda i, j: (layer, 0, j)),
                  pl.BlockSpec((tm, tn), lambda i, j: (i, j))],
        out_specs=pl.BlockSpec((tm, tn), lambda i, j: (i, j)),
        scratch_shapes=[pltpu.VMEM((tm, mix), BF16)],
        compiler_params=_params(("parallel", "arbitrary")),
        name="out_proj",
    )(y_hy, y_mla, y_d0, y_d1, y_d2, g.reshape(1, mix), w, x)


def _ffn_kernel(x_ref, g_ref, wu_ref, wd_ref, o_ref, h_ref):
    @pl.when(pl.program_id(1) == 0)
    def _():
        x = x_ref[...]
        h_ref[...] = (x * _rms_scale(x, x.shape[-1]) * g_ref[...]).astype(BF16)
        o_ref[...] = x

    a = jnp.maximum(jnp.dot(h_ref[...], wu_ref[...], preferred_element_type=F32), 0.0)
    o_ref[...] += jnp.dot((a * a).astype(BF16), wd_ref[...], preferred_element_type=F32)


def ffn(x, g, w_up, w_down, layer, tm=512, tf=512):
    m, d = x.shape
    f = w_up.shape[2]
    return pl.pallas_call(
        _ffn_kernel,
        out_shape=jax.ShapeDtypeStruct((m, d), F32),
        grid=(m // tm, f // tf),
        in_specs=[pl.BlockSpec((tm, d), lambda i, j: (i, 0), pipeline_mode=pl.Buffered(1)),
                  pl.BlockSpec((1, d), lambda i, j: (0, 0)),
                  pl.BlockSpec((None, d, tf), lambda i, j: (layer, 0, j)),
                  pl.BlockSpec((None, tf, d), lambda i, j: (layer, j, 0))],
        out_specs=pl.BlockSpec((tm, d), lambda i, j: (i, 0)),
        scratch_shapes=[pltpu.VMEM((tm, d), BF16)],
        compiler_params=_params(("parallel", "arbitrary")),
        name="ffn",
    )(x, g.reshape(1, d), w_up, w_down)


def _rope_tables(pos, dim):
    inv = 1.0 / (ROPE_THETA ** (jnp.arange(0, dim, 2, dtype=F32) / dim))
    ang = pos.astype(F32)[:, None] * inv[None, :]
    return jnp.cos(ang), jnp.sin(ang)


def _spread_halves(a):
    z = jnp.zeros(a.shape[:-1] + (ROPE_DIM // 2,), a.dtype)
    return jnp.concatenate([a[..., :ROPE_DIM // 2], z, a[..., ROPE_DIM // 2:], z], axis=-1)


def _rotate_half(x, cos, sin_signed):
    return x * cos + pltpu.roll(x, LANES // 2, axis=1) * sin_signed


def _mla_q_kernel(cq_ref, ga_ref, w_ref, gn_ref, cos_ref, sin_ref, q_ref, c_ref):
    @pl.when(pl.program_id(1) == 0)
    def _():
        c = cq_ref[...]
        c_ref[...] = (c * _rms_scale(c, Q_LORA) * ga_ref[...]).astype(BF16)

    qs = jnp.dot(c_ref[...], w_ref[...], preferred_element_type=F32)
    scale = QK_DIM ** -0.5
    for h in range(HEAD_GROUP):
        q = qs[:, h * 2 * LANES:(h + 1) * 2 * LANES]
        qn = q[:, :HEAD_DIM]
        qn = qn * _rms_scale(qn, HEAD_DIM) * gn_ref[:, :HEAD_DIM]
        qr = q[:, HEAD_DIM:]
        qr = qr * _rms_scale(qr, ROPE_DIM) * gn_ref[:, HEAD_DIM:]
        qr = _rotate_half(qr, cos_ref[...], sin_ref[...])
        q_ref[h, :, :HEAD_DIM] = (qn * scale).astype(BF16)
        q_ref[h, :, HEAD_DIM:] = (qr * scale).astype(BF16)


def mla_q(z, ga, w, gn, cos, sin, tm=512):
    m = z.shape[0]
    wide = HEAD_GROUP * 2 * LANES
    return pl.pallas_call(
        _mla_q_kernel,
        out_shape=jax.ShapeDtypeStruct((HEADS, m, 2 * LANES), BF16),
        grid=(m // tm, HEADS // HEAD_GROUP),
        in_specs=[pl.BlockSpec((tm, Q_LORA), lambda i, h: (i, COL_CQ // Q_LORA)),
                  pl.BlockSpec((1, Q_LORA), lambda i, h: (0, 0)),
                  pl.BlockSpec((None, Q_LORA, wide), lambda i, h: (h, 0, 0)),
                  pl.BlockSpec((1, 2 * LANES), lambda i, h: (0, 0)),
                  pl.BlockSpec((tm, LANES), lambda i, h: (i, 0)),
                  pl.BlockSpec((tm, LANES), lambda i, h: (i, 0))],
        out_specs=pl.BlockSpec((HEAD_GROUP, tm, 2 * LANES), lambda i, h: (h, i, 0)),
        scratch_shapes=[pltpu.VMEM((tm, Q_LORA), BF16)],
        compiler_params=_params(("parallel", "arbitrary")),
        name="mla_q",
    )(z, ga, w, gn, cos, sin)


def _mla_kv_kernel(ckv_ref, kr_ref, ga_ref, w_ref, gk_ref, gr_ref, cos_ref, sin_ref,
                   k_ref, v_ref, c_ref, r_ref):
    @pl.when(pl.program_id(1) == 0)
    def _():
        c = ckv_ref[...]
        c_ref[...] = (c * _rms_scale(c, KV_LORA) * ga_ref[...]).astype(BF16)
        kr = kr_ref[...]
        kr = kr * _rms_scale(kr, ROPE_DIM) * gr_ref[...]
        r_ref[...] = _rotate_half(kr, cos_ref[...], sin_ref[...]).astype(BF16)

    kvs = jnp.dot(c_ref[...], w_ref[...], preferred_element_type=F32)
    for h in range(HEAD_GROUP):
        kv = kvs[:, h * 2 * LANES:(h + 1) * 2 * LANES]
        kn = kv[:, :HEAD_DIM]
        k_ref[h, :, :HEAD_DIM] = (kn * _rms_scale(kn, HEAD_DIM) * gk_ref[...]).astype(BF16)
        k_ref[h, :, HEAD_DIM:] = r_ref[...]
        v_ref[h] = kv[:, HEAD_DIM:].astype(BF16)


def mla_kv(z, ga, w, gk, gr, cos, sin, tm=512):
    m = z.shape[0]
    wide = HEAD_GROUP * 2 * LANES
    return pl.pallas_call(
        _mla_kv_kernel,
        out_shape=(jax.ShapeDtypeStruct((HEADS, m, 2 * LANES), BF16),
                   jax.ShapeDtypeStruct((HEADS, m, LANES), BF16)),
        grid=(m // tm, HEADS // HEAD_GROUP),
        in_specs=[pl.BlockSpec((tm, KV_LORA), lambda i, h: (i, COL_CKV // KV_LORA)),
                  pl.BlockSpec((tm, LANES), lambda i, h: (i, COL_KR // LANES)),
                  pl.BlockSpec((1, KV_LORA), lambda i, h: (0, 0)),
                  pl.BlockSpec((None, KV_LORA, wide), lambda i, h: (h, 0, 0)),
                  pl.BlockSpec((1, LANES), lambda i, h: (0, 0)),
                  pl.BlockSpec((1, LANES), lambda i, h: (0, 0)),
                  pl.BlockSpec((tm, LANES), lambda i, h: (i, 0)),
                  pl.BlockSpec((tm, LANES), lambda i, h: (i, 0))],
        out_specs=(pl.BlockSpec((HEAD_GROUP, tm, 2 * LANES), lambda i, h: (h, i, 0)),
                   pl.BlockSpec((HEAD_GROUP, tm, LANES), lambda i, h: (h, i, 0))),
        scratch_shapes=[pltpu.VMEM((tm, KV_LORA), BF16), pltpu.VMEM((tm, LANES), BF16)],
        compiler_params=_params(("parallel", "arbitrary")),
        name="mla_kv",
    )(z, z, ga, w, gk, gr, cos, sin)


def _flash_kernel(q_ref, k_ref, v_ref, o_ref, m_ref, l_ref, acc_ref, *, nk, sub):
    j = pl.program_id(3)

    @pl.when(j == 0)
    def _():
        m_ref[...] = jnp.full(m_ref.shape, -jnp.inf, F32)
        l_ref[...] = jnp.zeros(l_ref.shape, F32)
        acc_ref[...] = jnp.zeros(acc_ref.shape, F32)

    k, v = k_ref[...], v_ref[...]
    for r in range(0, q_ref.shape[0], sub):
        rows = slice(r, r + sub)
        s = lax.dot_general(q_ref[rows, :], k, (((1,), (1,)), ((), ())),
                            preferred_element_type=F32)
        m = m_ref[rows, :]
        m_new = jnp.maximum(m, jnp.max(s, axis=-1, keepdims=True))
        alpha = jnp.exp(m - m_new)
        p = jnp.exp(s - m_new)
        l_ref[rows, :] = alpha * l_ref[rows, :] + jnp.sum(p, axis=-1, keepdims=True)
        acc_ref[rows, :] = alpha * acc_ref[rows, :] + jnp.dot(p.astype(BF16), v,
                                                              preferred_element_type=F32)
        m_ref[rows, :] = m_new

    @pl.when(j == nk - 1)
    def _():
        o_ref[...] = acc_ref[...] / l_ref[...]


def mla_attention(q, k, v, batch, seq, tq=1024, tk=2048, sub=512):
    tq, tk = min(tq, seq), min(tk, seq)
    sub = min(sub, tq)
    nq, nk = seq // tq, seq // tk
    return pl.pallas_call(
        functools.partial(_flash_kernel, nk=nk, sub=sub),
        out_shape=jax.ShapeDtypeStruct((batch * seq, GROUP_W), F32),
        grid=(batch, HEADS, nq, nk),
        in_specs=[pl.BlockSpec((None, tq, 2 * LANES), lambda b, h, i, j: (h, b * nq + i, 0)),
                  pl.BlockSpec((None, tk, 2 * LANES), lambda b, h, i, j: (h, b * nk + j, 0)),
                  pl.BlockSpec((None, tk, LANES), lambda b, h, i, j: (h, b * nk + j, 0))],
        out_specs=pl.BlockSpec((tq, LANES), lambda b, h, i, j: (b * nq + i, h)),
        scratch_shapes=[pltpu.VMEM((tq, 1), F32), pltpu.VMEM((tq, 1), F32),
                        pltpu.VMEM((tq, LANES), F32)],
        compiler_params=_params(("parallel", "parallel", "parallel", "arbitrary")),
        name="mla_attention",
    )(q, k, v)


def _dil_prep_kernel(q_ref, k_ref, v_ref, gq_ref, gk_ref, cos_ref, sin_ref, qo_ref, ko_ref, vo_ref):
    cos, sin = cos_ref[...], sin_ref[...]
    scale = HEAD_DIM ** -0.5
    for h in range(HEADS):
        sl = slice(h * HEAD_DIM, (h + 1) * HEAD_DIM)
        q = q_ref[:, sl]
        q = _rotate_half(q * _rms_scale(q, HEAD_DIM) * gq_ref[...], cos, sin)
        qo_ref[:, sl] = (q * scale).astype(BF16)
        k = k_ref[:, sl]
        k = _rotate_half(k * _rms_scale(k, HEAD_DIM) * gk_ref[...], cos, sin)
        ko_ref[:, sl] = k.astype(BF16)
    vo_ref[...] = v_ref[...].astype(BF16)


def dil_prep(z, gq, gk, cos, sin, tm=512):
    m = z.shape[0]
    sec = lambda c: pl.BlockSpec((tm, GROUP_W), lambda i: (i, c // GROUP_W))
    vec = pl.BlockSpec((1, LANES), lambda i: (0, 0))
    tab = pl.BlockSpec((tm, LANES), lambda i: (i, 0))
    out = pl.BlockSpec((tm, GROUP_W), lambda i: (i, 0))
    shp = jax.ShapeDtypeStruct((m, GROUP_W), BF16)
    return pl.pallas_call(
        _dil_prep_kernel,
        out_shape=(shp, shp, shp),
        grid=(m // tm,),
        in_specs=[sec(COL_DQ), sec(COL_DK), sec(COL_DV), vec, vec, tab, tab],
        out_specs=(out, out, out),
        compiler_params=_params(("parallel",)),
        name="dil_prep",
    )(z, z, z, gq, gk, cos, sin)


def _dil_attn_kernel(*refs, seq, tq):
    q_refs, k_refs, v_refs, o_refs = refs[0:3], refs[3:6], refs[6:9], refs[9:12]
    i = pl.program_id(2)
    outs, lses = [], []
    for g, (window, dil) in enumerate(DIL_PAIRS):
        reach = window // 2
        win = min(seq, tq + 2 * reach)
        start = jnp.clip(i * tq - reach, 0, seq - win)
        start = pl.multiple_of(start, 64)
        k = k_refs[g][pl.ds(start, win), :]
        v = v_refs[g][pl.ds(start, win), :]
        s = lax.dot_general(q_refs[g][...], k, (((1,), (1,)), ((), ())),
                            preferred_element_type=F32)
        qpos = i * tq + lax.broadcasted_iota(jnp.int32, (tq, win), 0)
        kpos = start + lax.broadcasted_iota(jnp.int32, (tq, win), 1)
        diff = kpos - qpos
        valid = (jnp.abs(diff) <= reach) & ((diff & (dil - 1)) == 0)
        s = jnp.where(valid, s, NEG)
        m = jnp.max(s, axis=-1, keepdims=True)
        p = jnp.exp(s - m)
        l = jnp.sum(p, axis=-1, keepdims=True)
        o = jnp.dot(p.astype(BF16), v, preferred_element_type=F32) / l
        outs.append(o)
        lses.append(m + jnp.log(l))
    top = jnp.maximum(jnp.maximum(lses[0], lses[1]), lses[2])
    es = [jnp.exp(t - top) for t in lses]
    den = es[0] + es[1] + es[2]
    for g in range(3):
        o_refs[g][...] = outs[g] * (es[g] / den)


def dil_attention(qd, kd, vd, batch, seq, tq=256):
    tq = min(tq, seq)
    nq = seq // tq
    qs = [pl.BlockSpec((tq, HEAD_DIM), functools.partial(
        lambda b, s, i, g: (b * nq + i, g * DIL_SLOTS + s), g=g)) for g in range(3)]
    ks = [pl.BlockSpec((seq, HEAD_DIM), functools.partial(
        lambda b, s, i, g: (b, g * DIL_SLOTS + s), g=g)) for g in range(3)]
    os_ = [pl.BlockSpec((tq, HEAD_DIM), lambda b, s, i: (b * nq + i, s)) for _ in range(3)]
    shp = jax.ShapeDtypeStruct((batch * seq, DIL_SLOTS * HEAD_DIM), F32)
    return pl.pallas_call(
        functools.partial(_dil_attn_kernel, seq=seq, tq=tq),
        out_shape=(shp, shp, shp),
        grid=(batch, DIL_SLOTS, nq),
        in_specs=qs + ks + ks,
        out_specs=tuple(os_),
        compiler_params=_params(("parallel", "parallel", "arbitrary")),
        name="dil_attention",
    )(qd, qd, qd, kd, kd, kd, vd, vd, vd)


def _hy_filter_kernel(z_ref, w1_ref, b1_ref, w2_ref, b2_ref, w3_ref, b3_ref, w4_ref, b4_ref,
                      fq_ref, dl_ref, o_ref, *, tl):
    fq = fq_ref[...]
    h = z_ref[...]
    for w_ref, b_ref in ((w1_ref, b1_ref), (w2_ref, b2_ref), (w3_ref, b3_ref)):
        h = jnp.sin(fq * (jnp.dot(h.astype(BF16), w_ref[...], preferred_element_type=F32) + b_ref[...]))
    h = jnp.dot(h.astype(BF16), w4_ref[...], preferred_element_type=F32) + b4_ref[...]
    t = z_ref[:, 0:1]
    decay = jnp.exp(-t * dl_ref[...])
    o_ref[0] = h[:, :HY_CH] * decay
    n = pl.program_id(0) * tl + lax.broadcasted_iota(jnp.int32, (tl, 1), 0)
    o_ref[1] = jnp.where(n == 0, 0.0, h[:, HY_CH:] * decay)


def hy_filter(zfeat, w1, b1, w2, b2, w3, b3, w4, b4, fq, deltas, tl=512):
    seq = zfeat.shape[0]
    full = lambda a: pl.BlockSpec(a.shape, lambda i: (0,) * a.ndim)
    args = (w1, b1, w2, b2, w3, b3, w4, b4, fq, deltas)
    return pl.pallas_call(
        functools.partial(_hy_filter_kernel, tl=tl),
        out_shape=jax.ShapeDtypeStruct((2, seq, HY_CH), F32),
        grid=(seq // tl,),
        in_specs=[pl.BlockSpec((tl, LANES), lambda i: (i, 0))] + [full(a) for a in args],
        out_specs=pl.BlockSpec((2, tl, HY_CH), lambda i: (0, i, 0)),
        compiler_params=_params(("parallel",)),
        name="hy_filter",
    )(zfeat, *args)


def _hy_pre_kernel(x0_ref, x1_ref, v_ref, p0_ref, p1_ref, pv_ref, n0_ref, n1_ref, nv_ref,
                   w_ref, b_ref, x0o_ref, u_ref, *, tl, blocks_per_seq):
    i = pl.program_id(0)
    first = (i % blocks_per_seq) == 0
    last = (i % blocks_per_seq) == blocks_per_seq - 1
    row = lax.broadcasted_iota(jnp.int32, (tl, 1), 0)

    def conv(c_ref, p_ref, n_ref, part):
        c = c_ref[...]
        sl = slice(part * HY_CH, (part + 1) * HY_CH)
        prev_row = jnp.where(first, 0.0, p_ref[7:8, :])
        next_row = jnp.where(last, 0.0, n_ref[0:1, :])
        down = jnp.where(row == 0, prev_row, pltpu.roll(c, 1, axis=0))
        up = jnp.where(row == tl - 1, next_row, pltpu.roll(c, tl - 1, axis=0))
        return down * w_ref[0:1, sl] + c * w_ref[1:2, sl] + up * w_ref[2:3, sl] + b_ref[:, sl]

    x0o_ref[...] = conv(x0_ref, p0_ref, n0_ref, 0)
    u_ref[...] = conv(x1_ref, p1_ref, n1_ref, 1) * conv(v_ref, pv_ref, nv_ref, 2)


def hy_pre(z, conv_w, conv_b, seq, tl=512):
    m = z.shape[0]
    nblk = m // tl
    r8 = tl // 8
    c0 = COL_HY // HY_CH
    cur = lambda p: pl.BlockSpec((tl, HY_CH), lambda i: (i, c0 + p))
    prv = lambda p: pl.BlockSpec((8, HY_CH), lambda i: (jnp.maximum(i * r8 - 1, 0), c0 + p))
    nxt = lambda p: pl.BlockSpec((8, HY_CH), lambda i: (jnp.minimum((i + 1) * r8, m // 8 - 1), c0 + p))
    out = pl.BlockSpec((tl, HY_CH), lambda i: (i, 0))
    shp = jax.ShapeDtypeStruct((m, HY_CH), F32)
    return pl.pallas_call(
        functools.partial(_hy_pre_kernel, tl=tl, blocks_per_seq=seq // tl),
        out_shape=(shp, shp),
        grid=(nblk,),
        in_specs=[cur(0), cur(1), cur(2), prv(0), prv(1), prv(2), nxt(0), nxt(1), nxt(2),
                  pl.BlockSpec((3, 3 * HY_CH), lambda i: (0, 0)),
                  pl.BlockSpec((1, 3 * HY_CH), lambda i: (0, 0))],
        out_specs=(out, out),
        compiler_params=_params(("parallel",)),
        name="hy_pre",
    )(z, z, z, z, z, z, z, z, z, conv_w, conv_b)


def _buffering(block_bytes):
    return dict(pipeline_mode=pl.Buffered(1)) if block_bytes > 8 * 1024 * 1024 else {}


def _dft_slow_kernel(a_ref, x_ref, o_ref, *, p, ph):
    for q in range(DFT_Q):
        xq = x_ref[pl.ds(q, ph, stride=DFT_Q), :].astype(BF16)
        y = jnp.dot(a_ref[...], xq, preferred_element_type=F32)
        o_ref[0, pl.ds(q, p, stride=DFT_Q), :] = y[:p]
        o_ref[1, pl.ds(q, p, stride=DFT_Q), :] = y[p:]


def dft_slow(a, x, tc):
    bsz, seq, ch = x.shape
    p = a.shape[0] // 2
    n = p * DFT_Q
    return pl.pallas_call(
        functools.partial(_dft_slow_kernel, p=p, ph=p // 2),
        out_shape=jax.ShapeDtypeStruct((bsz, 2, n, ch), F32),
        grid=(bsz, ch // tc),
        in_specs=[pl.BlockSpec(a.shape, lambda b, c: (0, 0)),
                  pl.BlockSpec((None, seq, tc), lambda b, c: (b, 0, c))],
        out_specs=pl.BlockSpec((None, 2, n, tc), lambda b, c: (b, 0, 0, c)),
        compiler_params=_params(("parallel", "parallel")),
        name="dft_slow",
    )(a, x)


def _cmul(ar, ai, br, bi):
    return ar * br - ai * bi, ar * bi + ai * br


def _dft_q(mat_ref, xr, xi):
    y = jnp.dot(mat_ref[...], jnp.concatenate([xr, xi], axis=0).astype(BF16),
                preferred_element_type=F32)
    return y[:DFT_Q], y[DFT_Q:]


SLABS = 4


def _hy_kspec_kernel(g_ref, tw_ref, fwd_ref, o_ref):
    for s in range(SLABS):
        rows = slice(s * DFT_Q, (s + 1) * DFT_Q)
        tr, ti = tw_ref[s, 0], tw_ref[s, 1]
        fr, fi = _dft_q(fwd_ref, *_cmul(g_ref[0, 0, rows, :], g_ref[0, 1, rows, :], tr, ti))
        br, bi = _dft_q(fwd_ref, *_cmul(g_ref[1, 0, rows, :], g_ref[1, 1, rows, :], tr, ti))
        o_ref[0, rows, :] = fr + br
        o_ref[1, rows, :] = fi - bi


def hy_kspec(g, tw, fwd, tc=512):
    n = g.shape[2]
    rows = SLABS * DFT_Q
    return pl.pallas_call(
        _hy_kspec_kernel,
        out_shape=jax.ShapeDtypeStruct((2, n, HY_CH), F32),
        grid=(n // rows, HY_CH // tc),
        in_specs=[pl.BlockSpec((2, 2, rows, tc), lambda kp, c: (0, 0, kp, c)),
                  pl.BlockSpec((SLABS, 2, DFT_Q, 1), lambda kp, c: (kp, 0, 0, 0)),
                  pl.BlockSpec((2 * DFT_Q, 2 * DFT_Q), lambda kp, c: (0, 0))],
        out_specs=pl.BlockSpec((2, rows, tc), lambda kp, c: (0, kp, c)),
        compiler_params=_params(("parallel", "parallel")),
        name="hy_kspec",
    )(g, tw, fwd)


def _hy_mid_kernel(g_ref, tw_ref, fwd_ref, inv_ref, ks_ref, o_ref):
    for s in range(SLABS):
        rows = slice(s * DFT_Q, (s + 1) * DFT_Q)
        tr, ti = tw_ref[s, 0], tw_ref[s, 1]
        xr, xi = _dft_q(fwd_ref, *_cmul(g_ref[0, rows, :], g_ref[1, rows, :], tr, ti))
        yr, yi = _cmul(xr, xi, ks_ref[0, rows, :], ks_ref[1, rows, :])
        wr, wi = _dft_q(inv_ref, yr, yi)
        o_ref[0, rows, :], o_ref[1, rows, :] = _cmul(wr, wi, tr, -ti)


def hy_mid(g, tw, fwd, inv, kspec, tc=512):
    bsz, _, n, _ = g.shape
    rows = SLABS * DFT_Q
    return pl.pallas_call(
        _hy_mid_kernel,
        out_shape=jax.ShapeDtypeStruct(g.shape, F32),
        grid=(bsz, n // rows, HY_CH // tc),
        in_specs=[pl.BlockSpec((None, 2, rows, tc), lambda b, kp, c: (b, 0, kp, c)),
                  pl.BlockSpec((SLABS, 2, DFT_Q, 1), lambda b, kp, c: (kp, 0, 0, 0)),
                  pl.BlockSpec((2 * DFT_Q, 2 * DFT_Q), lambda b, kp, c: (0, 0)),
                  pl.BlockSpec((2 * DFT_Q, 2 * DFT_Q), lambda b, kp, c: (0, 0)),
                  pl.BlockSpec((2, rows, tc), lambda b, kp, c: (0, kp, c))],
        out_specs=pl.BlockSpec((None, 2, rows, tc), lambda b, kp, c: (b, 0, kp, c)),
        compiler_params=_params(("parallel", "parallel", "parallel")),
        name="hy_mid",
    )(g, tw, fwd, inv, kspec)


def _hy_post_kernel(a_ref, h_ref, u_ref, x0_ref, skip_ref, o_ref, *, p, ph, inv_n):
    skip = skip_ref[...]
    for q in range(DFT_Q):
        hq = jnp.concatenate([h_ref[0, pl.ds(q, p, stride=DFT_Q), :],
                              h_ref[1, pl.ds(q, p, stride=DFT_Q), :]], axis=0).astype(BF16)
        y = jnp.dot(a_ref[...], hq, preferred_element_type=F32) * inv_n
        tq = pl.ds(q, ph, stride=DFT_Q)
        o_ref[tq, :] = x0_ref[tq, :] * (y + u_ref[tq, :] * skip)


def hy_post(a, h, u, x0, skip, tc):
    bsz, _, n, ch = h.shape
    seq = n // 2
    p = n // DFT_Q
    seq_spec = pl.BlockSpec((None, seq, tc), lambda b, c: (b, 0, c))
    return pl.pallas_call(
        functools.partial(_hy_post_kernel, p=p, ph=p // 2, inv_n=1.0 / n),
        out_shape=jax.ShapeDtypeStruct((bsz, seq, ch), F32),
        grid=(bsz, ch // tc),
        in_specs=[pl.BlockSpec(a.shape, lambda b, c: (0, 0)),
                  pl.BlockSpec((None, 2, n, tc), lambda b, c: (b, 0, 0, c), **_buffering(2 * n * tc * 4)),
                  seq_spec, seq_spec,
                  pl.BlockSpec((1, tc), lambda b, c: (0, c))],
        out_specs=seq_spec,
        compiler_params=_params(("parallel", "parallel")),
        name="hy_post",
    )(a, h, u, x0, skip)


def _dft_constants(seq):
    n = 2 * seq
    q = DFT_Q
    p = n // q
    ph = p // 2
    fp = np.exp(-2j * np.pi * np.outer(np.arange(p), np.arange(p)) / p)
    fq = np.exp(-2j * np.pi * np.outer(np.arange(q), np.arange(q)) / q)
    tw = np.exp(-2j * np.pi * np.outer(np.arange(p), np.arange(q)) / n)
    a1 = np.concatenate([fp.real[:, :ph], fp.imag[:, :ph]], axis=0)
    fwd = np.block([[fq.real, -fq.imag], [fq.imag, fq.real]])
    inv = np.block([[fq.real, fq.imag], [-fq.imag, fq.real]])
    a3 = np.concatenate([fp.real[:ph, :], fp.imag[:ph, :]], axis=1)
    twa = np.stack([tw.real, tw.imag], axis=1)[..., None]
    cast = lambda a: jnp.asarray(a, F32).astype(BF16)
    return dict(n=n, p=p, ph=ph, a1=cast(a1), fwd=cast(fwd), inv=cast(inv), a3=cast(a3),
                tw=jnp.asarray(twa, F32))


def _hy_features(seq):
    t = jnp.linspace(0.0, 1.0, seq, dtype=F32)[:, None]
    w = 2.0 * math.pi * jnp.arange(seq, dtype=F32)[:, None] / seq
    f = jnp.linspace(1e-4, HY_BANDS - 1, HY_BANDS, dtype=F32)[None, :]
    z = jnp.concatenate([t, jnp.cos(f * w), -jnp.sin(f * w)], axis=-1)
    return jnp.pad(z, ((0, 0), (0, LANES - HY_EMB)))


def _hy_deltas():
    max_decay = math.log(1e-2) / 0.3
    min_decay = math.log(1e-2) / 1.5
    return jnp.abs(jnp.linspace(min_decay, max_decay, HY_CH, dtype=F32))[None, :]


def _pad2(a, rows, cols, value=0.0):
    return jnp.pad(a, ((0, rows - a.shape[0]), (0, cols - a.shape[1])), constant_values=value)


def hyena_spectrum(seq, consts, fw):
    h = hy_filter(_hy_features(seq), *fw, _hy_deltas(), tl=min(512, seq))
    g = dft_slow(consts["a1"], h, _hy_channel_tile(seq))
    return hy_kspec(g, consts["tw"], consts["fwd"])


def _hy_channel_tile(seq):
    del seq
    return LANES


def hyena_conv(x0, u, batch, seq, consts, kspec, skip):
    tc = _hy_channel_tile(seq)
    uv = u.reshape(batch, seq, HY_CH)
    g = dft_slow(consts["a1"], uv, tc)
    hm = hy_mid(g, consts["tw"], consts["fwd"], consts["inv"], kspec)
    y = hy_post(consts["a3"], hm, uv, x0.reshape(batch, seq, HY_CH), skip, tc)
    return y.reshape(batch * seq, HY_CH)


def _layer_params(l, norm_mix, w_in, hy_conv_w, hy_conv_b, hy_f_w1, hy_f_b1, hy_f_w2, hy_f_b2, hy_f_w3,
                  hy_f_b3, hy_f_w4, hy_f_b4, hy_f_freq, hy_skip, mla_q_a_norm, mla_w_q_b, mla_kv_a_norm,
                  mla_w_kv_b, mla_qn_nope, mla_qn_rope, mla_kn_nope, mla_kn_rope, dil_q_norm, dil_k_norm,
                  out_norm, w_out, norm_ffn, w_up, w_down):
    d = w_in.shape[1]
    wi = w_in[l]
    hy_cols = 3 * HY_CH
    o_cq = hy_cols
    o_ckv = o_cq + Q_LORA
    o_kr = o_ckv + KV_LORA
    o_dil = o_kr + ROPE_DIM
    w_cat = jnp.concatenate([
        wi[:, o_dil:], wi[:, o_cq:o_ckv], wi[:, :hy_cols], wi[:, o_ckv:o_kr],
        _spread_halves(wi[:, o_kr:o_dil]), jnp.zeros((d, Z_COLS - Z_USED), wi.dtype)], axis=1).astype(BF16)
    wq = mla_w_q_b[l].reshape(Q_LORA, HEADS, QK_DIM)
    wq = jnp.concatenate([wq[..., :HEAD_DIM], _spread_halves(wq[..., HEAD_DIM:])], axis=-1)
    wide = HEAD_GROUP * 2 * LANES
    wq = wq.reshape(Q_LORA, HEADS // HEAD_GROUP, wide).transpose(1, 0, 2).astype(BF16)
    wkv = mla_w_kv_b[l].reshape(KV_LORA, HEADS // HEAD_GROUP, wide).transpose(1, 0, 2).astype(BF16)
    hf = LANES
    filt = (_pad2(hy_f_w1[l], hf, hf).astype(BF16), _pad2(hy_f_b1[l][None], 1, hf),
            _pad2(hy_f_w2[l], hf, hf).astype(BF16), _pad2(hy_f_b2[l][None], 1, hf),
            _pad2(hy_f_w3[l], hf, hf).astype(BF16), _pad2(hy_f_b3[l][None], 1, hf),
            _pad2(hy_f_w4[l], hf, 2 * HY_CH).astype(BF16), hy_f_b4[l][None],
            _pad2(hy_f_freq[l][None], 1, hf, 1.0))
    return dict(
        norm_mix=norm_mix[l], w_in=w_cat,
        conv_w=hy_conv_w[l], conv_b=hy_conv_b[l][None], filt=filt,
        skip=hy_skip[l][None],
        q_a_norm=mla_q_a_norm[l][None], w_q=wq,
        q_gain=jnp.concatenate([mla_qn_nope[l], _spread_halves(mla_qn_rope[l])])[None],
        kv_a_norm=mla_kv_a_norm[l][None], w_kv=wkv,
        kn_nope=mla_kn_nope[l][None], kn_rope=_spread_halves(mla_kn_rope[l])[None],
        dil_q=dil_q_norm[l][None], dil_k=dil_k_norm[l][None],
        out_norm=out_norm[l], norm_ffn=norm_ffn[l], layer=l)


def mixers(z, p, batch, seq, consts, tabs):
    mla_cos, mla_sin, dil_cos, dil_sin = tabs
    x0, u = hy_pre(z, p["conv_w"], p["conv_b"], seq)
    kspec = hyena_spectrum(seq, consts, p["filt"])
    y_hy = hyena_conv(x0, u, batch, seq, consts, kspec, p["skip"])
    q = mla_q(z, p["q_a_norm"], p["w_q"], p["q_gain"], mla_cos, mla_sin)
    k, v = mla_kv(z, p["kv_a_norm"], p["w_kv"], p["kn_nope"], p["kn_rope"], mla_cos, mla_sin)
    y_mla = mla_attention(q, k, v, batch, seq)
    qd, kd, vd = dil_prep(z, p["dil_q"], p["dil_k"], dil_cos, dil_sin)
    return (y_hy, y_mla) + tuple(dil_attention(qd, kd, vd, batch, seq))


def rope_tabs(batch, seq):
    pos = jnp.tile(jnp.arange(seq, dtype=jnp.int32), batch)
    c64, s64 = _rope_tables(pos, ROPE_DIM)
    c128, s128 = _rope_tables(pos, HEAD_DIM)
    return (_spread_halves(jnp.concatenate([c64, c64], -1)),
            _spread_halves(jnp.concatenate([-s64, s64], -1)),
            jnp.concatenate([c128, c128], -1), jnp.concatenate([-s128, s128], -1))


def _trunk(xin, layers, shared):
    batch, seq, d = xin.shape
    x = xin.reshape(batch * seq, d)
    tabs = rope_tabs(batch, seq)
    consts = _dft_constants(seq)
    for p in layers:
        z = norm_matmul(x, p["norm_mix"], p["w_in"])
        ys = mixers(z, p, batch, seq, consts, tabs)
        x = out_proj(*ys, p["out_norm"], shared["w_out"], x, p["layer"])
        x = ffn(x, p["norm_ffn"], shared["w_up"], shared["w_down"], p["layer"])
    return x.reshape(xin.shape)


def kernel(x_prompt, x_sample, norm_mix, w_in, hy_conv_w, hy_conv_b, hy_f_w1, hy_f_b1, hy_f_w2, hy_f_b2, hy_f_w3, hy_f_b3, hy_f_w4, hy_f_b4, hy_f_freq, hy_skip, mla_q_a_norm, mla_w_q_b, mla_kv_a_norm, mla_w_kv_b, mla_qn_nope, mla_qn_rope, mla_kn_nope, mla_kn_rope, dil_q_norm, dil_k_norm, out_norm, w_out, norm_ffn, w_up, w_down):
    weights = (norm_mix, w_in, hy_conv_w, hy_conv_b, hy_f_w1, hy_f_b1, hy_f_w2, hy_f_b2, hy_f_w3, hy_f_b3,
               hy_f_w4, hy_f_b4, hy_f_freq, hy_skip, mla_q_a_norm, mla_w_q_b, mla_kv_a_norm, mla_w_kv_b,
               mla_qn_nope, mla_qn_rope, mla_kn_nope, mla_kn_rope, dil_q_norm, dil_k_norm, out_norm, w_out,
               norm_ffn, w_up, w_down)
    layers = [_layer_params(l, *weights) for l in range(norm_mix.shape[0])]
    shared = dict(w_out=w_out.astype(BF16), w_up=w_up.astype(BF16), w_down=w_down.astype(BF16))
    return tuple(_trunk(xin, layers, shared) for xin in (x_prompt, x_sample))
```

```python
import functools
import math

import numpy as np
import jax
import jax.numpy as jnp
from jax import lax
from jax.experimental import pallas as pl
from jax.experimental.pallas import tpu as pltpu

F32 = jnp.float32
BF16 = jnp.bfloat16

EPS = 1e-6
NEG = -1e30
ROPE_THETA = 10000.0
LANES = 128
VMEM_LIMIT = 56 * 1024 * 1024

HY_CH = 1024
GROUP_W = 1536
HEADS = 12
HEAD_DIM = 128
ROPE_DIM = 64
QK_DIM = HEAD_DIM + ROPE_DIM
Q_LORA = 1536
KV_LORA = 512
DIL_PAIRS = ((128, 1), (512, 4), (2048, 16))
DIL_SLOTS = 4
HY_EMB = 33
HY_BANDS = 16
HY_FFN = 64
HEAD_GROUP = 4

COL_DQ, COL_DK, COL_DV = 0, 1536, 3072
COL_CQ = 4608
COL_HY = 6144
COL_CKV = 9216
COL_KR = 9728
Z_USED = 9856
Z_COLS = 10240
DFT_Q = 128


def _params(sem, vmem=VMEM_LIMIT):
    return pltpu.CompilerParams(dimension_semantics=sem, vmem_limit_bytes=vmem)


def _rms_scale(x, width):
    return lax.rsqrt(jnp.sum(x * x, axis=-1, keepdims=True) * (1.0 / width) + EPS)


NORM_ROWS = 256


def _norm_matmul_kernel(x_ref, g_ref, w_ref, o_ref, h_ref):
    @pl.when(pl.program_id(1) == 0)
    def _():
        for r in range(0, x_ref.shape[0], NORM_ROWS):
            x = x_ref[r:r + NORM_ROWS, :]
            h_ref[r:r + NORM_ROWS, :] = (x * _rms_scale(x, x.shape[-1]) * g_ref[...]).astype(BF16)

    o_ref[...] = jnp.dot(h_ref[...], w_ref[...], preferred_element_type=F32)


def norm_matmul(x, g, w, tm=1024, tn=512):
    m, d = x.shape
    n = w.shape[1]
    return pl.pallas_call(
        _norm_matmul_kernel,
        out_shape=jax.ShapeDtypeStruct((m, n), F32),
        grid=(m // tm, n // tn),
        in_specs=[pl.BlockSpec((tm, d), lambda i, j: (i, 0), pipeline_mode=pl.Buffered(1)),
                  pl.BlockSpec((1, d), lambda i, j: (0, 0)),
                  pl.BlockSpec((d, tn), lambda i, j: (0, j))],
        out_specs=pl.BlockSpec((tm, tn), lambda i, j: (i, j)),
        scratch_shapes=[pltpu.VMEM((tm, d), BF16)],
        compiler_params=_params(("parallel", "arbitrary")),
        name="norm_matmul",
    )(x, g.reshape(1, d), w)


def _out_proj_kernel(hy_ref, mla_ref, d0_ref, d1_ref, d2_ref, g_ref, w_ref, x_ref, o_ref, h_ref):
    @pl.when(pl.program_id(1) == 0)
    def _():
        base = HY_CH + GROUP_W
        gw = GROUP_W // 3
        for r0 in range(0, h_ref.shape[0], NORM_ROWS):
            rs = slice(r0, r0 + NORM_ROWS)
            hy = hy_ref[rs, :]
            h_ref[rs, :HY_CH] = (hy * _rms_scale(hy, HY_CH) * g_ref[:, :HY_CH]).astype(BF16)
            ml = mla_ref[rs, :].astype(F32)
            h_ref[rs, HY_CH:base] = (ml * _rms_scale(ml, GROUP_W) * g_ref[:, HY_CH:base]).astype(BF16)
            d0, d1, d2 = (r_[rs, :].astype(F32) for r_ in (d0_ref, d1_ref, d2_ref))
            ss = (jnp.sum(d0 * d0, axis=-1, keepdims=True) + jnp.sum(d1 * d1, axis=-1, keepdims=True)
                  + jnp.sum(d2 * d2, axis=-1, keepdims=True))
            r = lax.rsqrt(ss * (1.0 / GROUP_W) + EPS)
            for k, dk in enumerate((d0, d1, d2)):
                lo = base + k * gw
                h_ref[rs, lo:lo + gw] = (dk * r * g_ref[:, lo:lo + gw]).astype(BF16)

    o_ref[...] = x_ref[...] + jnp.dot(h_ref[...], w_ref[...], preferred_element_type=F32)


def out_proj(y_hy, y_mla, y_d0, y_d1, y_d2, g, w, x, layer, tm=1024, tn=512):
    m, d = x.shape
    mix = w.shape[1]
    gw = GROUP_W // 3
    row = lambda i, j: (i, 0)
    return pl.pallas_call(
        _out_proj_kernel,
        out_shape=jax.ShapeDtypeStruct((m, d), F32),
        grid=(m // tm, d // tn),
        in_specs=[pl.BlockSpec((tm, HY_CH), row),
                  pl.BlockSpec((tm, GROUP_W), row),
                  pl.BlockSpec((tm, gw), row),
                  pl.BlockSpec((tm, gw), row),
                  pl.BlockSpec((tm, gw), row),
                  pl.BlockSpec((1, mix), lambda i, j: (0, 0)),
                  pl.BlockSpec((None, mix, tn), lambda i, j: (layer, 0, j)),
                  pl.BlockSpec((tm, tn), lambda i, j: (i, j))],
        out_specs=pl.BlockSpec((tm, tn), lambda i, j: (i, j)),
        scratch_shapes=[pltpu.VMEM((tm, mix), BF16)],
        compiler_params=_params(("parallel", "arbitrary")),
        name="out_proj",
    )(y_hy, y_mla, y_d0, y_d1, y_d2, g.reshape(1, mix), w, x)


def _ffn_kernel(x_ref, g_ref, wu_ref, wd_ref, o_ref, h_ref):
    @pl.when(pl.program_id(1) == 0)
    def _():
        x = x_ref[...]
        h_ref[...] = (x * _rms_scale(x, x.shape[-1]) * g_ref[...]).astype(BF16)
        o_ref[...] = x

    a = jnp.maximum(jnp.dot(h_ref[...], wu_ref[...], preferred_element_type=F32), 0.0)
    o_ref[...] += jnp.dot((a * a).astype(BF16), wd_ref[...], preferred_element_type=F32)


def ffn(x, g, w_up, w_down, layer, tm=512, tf=512):
    m, d = x.shape
    f = w_up.shape[2]
    return pl.pallas_call(
        _ffn_kernel,
        out_shape=jax.ShapeDtypeStruct((m, d), F32),
        grid=(m // tm, f // tf),
        in_specs=[pl.BlockSpec((tm, d), lambda i, j: (i, 0), pipeline_mode=pl.Buffered(1)),
                  pl.BlockSpec((1, d), lambda i, j: (0, 0)),
                  pl.BlockSpec((None, d, tf), lambda i, j: (layer, 0, j)),
                  pl.BlockSpec((None, tf, d), lambda i, j: (layer, j, 0))],
        out_specs=pl.BlockSpec((tm, d), lambda i, j: (i, 0)),
        scratch_shapes=[pltpu.VMEM((tm, d), BF16)],
        compiler_params=_params(("parallel", "arbitrary")),
        name="ffn",
    )(x, g.reshape(1, d), w_up, w_down)


def _rope_tables(pos, dim):
    inv = 1.0 / (ROPE_THETA ** (jnp.arange(0, dim, 2, dtype=F32) / dim))
    ang = pos.astype(F32)[:, None] * inv[None, :]
    return jnp.cos(ang), jnp.sin(ang)


def _spread_halves(a):
    z = jnp.zeros(a.shape[:-1] + (ROPE_DIM // 2,), a.dtype)
    return jnp.concatenate([a[..., :ROPE_DIM // 2], z, a[..., ROPE_DIM // 2:], z], axis=-1)


def _rotate_half(x, cos, sin_signed):
    return x * cos + pltpu.roll(x, LANES // 2, axis=1) * sin_signed


def _mla_q_kernel(cq_ref, ga_ref, w_ref, gn_ref, cos_ref, sin_ref, q_ref, c_ref):
    @pl.when(pl.program_id(1) == 0)
    def _():
        c = cq_ref[...]
        c_ref[...] = (c * _rms_scale(c, Q_LORA) * ga_ref[...]).astype(BF16)

    qs = jnp.dot(c_ref[...], w_ref[...], preferred_element_type=F32)
    scale = QK_DIM ** -0.5 * math.log2(math.e)
    for h in range(HEAD_GROUP):
        q = qs[:, h * 2 * LANES:(h + 1) * 2 * LANES]
        qn = q[:, :HEAD_DIM]
        qn = qn * _rms_scale(qn, HEAD_DIM) * gn_ref[:, :HEAD_DIM]
        qr = q[:, HEAD_DIM:]
        qr = qr * _rms_scale(qr, ROPE_DIM) * gn_ref[:, HEAD_DIM:]
        qr = _rotate_half(qr, cos_ref[...], sin_ref[...])
        q_ref[h, :, :HEAD_DIM] = (qn * scale).astype(BF16)
        q_ref[h, :, HEAD_DIM:] = (qr * scale).astype(BF16)


def mla_q(z, ga, w, gn, cos, sin, tm=512):
    m = z.shape[0]
    wide = HEAD_GROUP * 2 * LANES
    return pl.pallas_call(
        _mla_q_kernel,
        out_shape=jax.ShapeDtypeStruct((HEADS, m, 2 * LANES), BF16),
        grid=(m // tm, HEADS // HEAD_GROUP),
        in_specs=[pl.BlockSpec((tm, Q_LORA), lambda i, h: (i, COL_CQ // Q_LORA)),
                  pl.BlockSpec((1, Q_LORA), lambda i, h: (0, 0)),
                  pl.BlockSpec((None, Q_LORA, wide), lambda i, h: (h, 0, 0)),
                  pl.BlockSpec((1, 2 * LANES), lambda i, h: (0, 0)),
                  pl.BlockSpec((tm, LANES), lambda i, h: (i, 0)),
                  pl.BlockSpec((tm, LANES), lambda i, h: (i, 0))],
        out_specs=pl.BlockSpec((HEAD_GROUP, tm, 2 * LANES), lambda i, h: (h, i, 0)),
        scratch_shapes=[pltpu.VMEM((tm, Q_LORA), BF16)],
        compiler_params=_params(("parallel", "arbitrary")),
        name="mla_q",
    )(z, ga, w, gn, cos, sin)


def _mla_kv_kernel(ckv_ref, kr_ref, ga_ref, w_ref, gk_ref, gr_ref, cos_ref, sin_ref,
                   k_ref, v_ref, c_ref, r_ref):
    @pl.when(pl.program_id(1) == 0)
    def _():
        c = ckv_ref[...]
        c_ref[...] = (c * _rms_scale(c, KV_LORA) * ga_ref[...]).astype(BF16)
        kr = kr_ref[...]
        kr = kr * _rms_scale(kr, ROPE_DIM) * gr_ref[...]
        r_ref[...] = _rotate_half(kr, cos_ref[...], sin_ref[...]).astype(BF16)

    kvs = jnp.dot(c_ref[...], w_ref[...], preferred_element_type=F32)
    lane = lax.broadcasted_iota(jnp.int32, (c_ref.shape[0], LANES), 1)
    ones_col = jnp.where(lane == 0, 1.0, 0.0).astype(BF16)
    for h in range(HEAD_GROUP):
        kv = kvs[:, h * 2 * LANES:(h + 1) * 2 * LANES]
        kn = kv[:, :HEAD_DIM]
        k_ref[h, :, :HEAD_DIM] = (kn * _rms_scale(kn, HEAD_DIM) * gk_ref[...]).astype(BF16)
        k_ref[h, :, HEAD_DIM:] = r_ref[...]
        v_ref[h, :, :HEAD_DIM] = kv[:, HEAD_DIM:].astype(BF16)
        v_ref[h, :, HEAD_DIM:] = ones_col


def mla_kv(z, ga, w, gk, gr, cos, sin, tm=512):
    m = z.shape[0]
    wide = HEAD_GROUP * 2 * LANES
    return pl.pallas_call(
        _mla_kv_kernel,
        out_shape=(jax.ShapeDtypeStruct((HEADS, m, 2 * LANES), BF16),
                   jax.ShapeDtypeStruct((HEADS, m, 2 * LANES), BF16)),
        grid=(m // tm, HEADS // HEAD_GROUP),
        in_specs=[pl.BlockSpec((tm, KV_LORA), lambda i, h: (i, COL_CKV // KV_LORA)),
                  pl.BlockSpec((tm, LANES), lambda i, h: (i, COL_KR // LANES)),
                  pl.BlockSpec((1, KV_LORA), lambda i, h: (0, 0)),
                  pl.BlockSpec((None, KV_LORA, wide), lambda i, h: (h, 0, 0)),
                  pl.BlockSpec((1, LANES), lambda i, h: (0, 0)),
                  pl.BlockSpec((1, LANES), lambda i, h: (0, 0)),
                  pl.BlockSpec((tm, LANES), lambda i, h: (i, 0)),
                  pl.BlockSpec((tm, LANES), lambda i, h: (i, 0))],
        out_specs=(pl.BlockSpec((HEAD_GROUP, tm, 2 * LANES), lambda i, h: (h, i, 0)),
                   pl.BlockSpec((HEAD_GROUP, tm, 2 * LANES), lambda i, h: (h, i, 0))),
        scratch_shapes=[pltpu.VMEM((tm, KV_LORA), BF16), pltpu.VMEM((tm, LANES), BF16)],
        compiler_params=_params(("parallel", "arbitrary")),
        name="mla_kv",
    )(z, z, ga, w, gk, gr, cos, sin)


def _flash_kernel(q_ref, k_ref, v_ref, o_ref, m_ref, acc_ref, *, nk, sub):
    j = pl.program_id(3)

    @pl.when(j == 0)
    def _():
        m_ref[...] = jnp.full(m_ref.shape, -jnp.inf, F32)
        acc_ref[...] = jnp.zeros(acc_ref.shape, F32)

    k, v = k_ref[...], v_ref[...]

    def scores(r):
        return lax.dot_general(q_ref[r:r + sub, :], k, (((1,), (1,)), ((), ())),
                               preferred_element_type=F32)

    starts = list(range(0, q_ref.shape[0], sub))
    s_next = scores(starts[0])
    for idx, r in enumerate(starts):
        rows = slice(r, r + sub)
        s = s_next
        if idx + 1 < len(starts):
            s_next = scores(starts[idx + 1])
        m = m_ref[rows, :]
        m_new = jnp.maximum(m, jnp.max(s, axis=-1, keepdims=True))
        alpha = jnp.exp2(m - m_new)
        p = jnp.exp2((s - m_new).astype(BF16))
        acc_ref[rows, :] = alpha * acc_ref[rows, :] + jnp.dot(p, v, preferred_element_type=F32)
        m_ref[rows, :] = m_new

    @pl.when(j == nk - 1)
    def _():
        o_ref[...] = (acc_ref[:, :HEAD_DIM] / acc_ref[:, HEAD_DIM:HEAD_DIM + 1]).astype(o_ref.dtype)


def mla_attention(q, k, v, batch, seq, tq=2048, tk=2048, sub=256):
    tq, tk = min(tq, seq), min(tk, seq)
    sub = min(sub, tq)
    nq, nk = seq // tq, seq // tk
    return pl.pallas_call(
        functools.partial(_flash_kernel, nk=nk, sub=sub),
        out_shape=jax.ShapeDtypeStruct((batch * seq, GROUP_W), BF16),
        grid=(batch, HEADS, nq, nk),
        in_specs=[pl.BlockSpec((None, tq, 2 * LANES), lambda b, h, i, j: (h, b * nq + i, 0)),
                  pl.BlockSpec((None, tk, 2 * LANES), lambda b, h, i, j: (h, b * nk + j, 0)),
                  pl.BlockSpec((None, tk, 2 * LANES), lambda b, h, i, j: (h, b * nk + j, 0))],
        out_specs=pl.BlockSpec((tq, LANES), lambda b, h, i, j: (b * nq + i, h)),
        scratch_shapes=[pltpu.VMEM((tq, 1), F32), pltpu.VMEM((tq, 2 * LANES), F32)],
        compiler_params=_params(("parallel", "parallel", "parallel", "arbitrary")),
        name="mla_attention",
    )(q, k, v)


def _dil_prep_kernel(q_ref, k_ref, v_ref, gq_ref, gk_ref, cos_ref, sin_ref, qo_ref, ko_ref, vo_ref):
    cos, sin = cos_ref[...], sin_ref[...]
    scale = HEAD_DIM ** -0.5 * math.log2(math.e)
    for h in range(HEADS):
        sl = slice(h * HEAD_DIM, (h + 1) * HEAD_DIM)
        q = q_ref[:, sl]
        q = _rotate_half(q * _rms_scale(q, HEAD_DIM) * gq_ref[...], cos, sin)
        qo_ref[:, sl] = (q * scale).astype(BF16)
        k = k_ref[:, sl]
        k = _rotate_half(k * _rms_scale(k, HEAD_DIM) * gk_ref[...], cos, sin)
        ko_ref[:, sl] = k.astype(BF16)
    vo_ref[...] = v_ref[...].astype(BF16)


def dil_prep(z, gq, gk, cos, sin, tm=512):
    m = z.shape[0]
    sec = lambda c: pl.BlockSpec((tm, GROUP_W), lambda i: (i, c // GROUP_W))
    vec = pl.BlockSpec((1, LANES), lambda i: (0, 0))
    tab = pl.BlockSpec((tm, LANES), lambda i: (i, 0))
    out = pl.BlockSpec((tm, GROUP_W), lambda i: (i, 0))
    shp = jax.ShapeDtypeStruct((m, GROUP_W), BF16)
    return pl.pallas_call(
        _dil_prep_kernel,
        out_shape=(shp, shp, shp),
        grid=(m // tm,),
        in_specs=[sec(COL_DQ), sec(COL_DK), sec(COL_DV), vec, vec, tab, tab],
        out_specs=(out, out, out),
        compiler_params=_params(("parallel",)),
        name="dil_prep",
    )(z, z, z, gq, gk, cos, sin)


def _dil_attn_kernel(*refs, seq, tq):
    q_refs, k_refs, v_refs, o_refs = refs[0:3], refs[3:6], refs[6:9], refs[9:12]
    i = pl.program_id(2)
    plan = []
    for g, (window, dil) in enumerate(DIL_PAIRS):
        reach = window // 2
        win = min(seq, tq + 2 * reach)
        start = jnp.clip(i * tq - reach, 0, seq - win)
        start = pl.multiple_of(start, 64)
        s = lax.dot_general(q_refs[g][...], k_refs[g][pl.ds(start, win), :], (((1,), (1,)), ((), ())),
                            preferred_element_type=F32)
        plan.append((g, dil, reach, win, start, s))
    outs, lses = [], []
    for g, dil, reach, win, start, s in plan:
        diff = (lax.broadcasted_iota(jnp.int32, (tq, win), 1)
                - lax.broadcasted_iota(jnp.int32, (tq, win), 0)) + (start - i * tq)
        valid = (jnp.abs(diff) <= reach) & ((diff & (dil - 1)) == 0)
        s = jnp.where(valid, s, NEG)
        m = jnp.max(s, axis=-1, keepdims=True)
        p = jnp.exp2(s - m)
        l = jnp.sum(p, axis=-1, keepdims=True)
        v = v_refs[g][pl.ds(start, win), :]
        outs.append(jnp.dot(p.astype(BF16), v, preferred_element_type=F32) / l)
        lses.append(m + jnp.log2(l))
    top = jnp.maximum(jnp.maximum(lses[0], lses[1]), lses[2])
    es = [jnp.exp2(t - top) for t in lses]
    den = es[0] + es[1] + es[2]
    for g in range(3):
        o_refs[g][...] = (outs[g] * (es[g] / den)).astype(o_refs[g].dtype)


def dil_attention(qd, kd, vd, batch, seq, tq=256):
    tq = min(tq, seq)
    nq = seq // tq
    qs = [pl.BlockSpec((tq, HEAD_DIM), functools.partial(
        lambda b, s, i, g: (b * nq + i, g * DIL_SLOTS + s), g=g)) for g in range(3)]
    ks = [pl.BlockSpec((seq, HEAD_DIM), functools.partial(
        lambda b, s, i, g: (b, g * DIL_SLOTS + s), g=g)) for g in range(3)]
    os_ = [pl.BlockSpec((tq, HEAD_DIM), lambda b, s, i: (b * nq + i, s)) for _ in range(3)]
    shp = jax.ShapeDtypeStruct((batch * seq, DIL_SLOTS * HEAD_DIM), BF16)
    return pl.pallas_call(
        functools.partial(_dil_attn_kernel, seq=seq, tq=tq),
        out_shape=(shp, shp, shp),
        grid=(batch, DIL_SLOTS, nq),
        in_specs=qs + ks + ks,
        out_specs=tuple(os_),
        compiler_params=_params(("parallel", "parallel", "arbitrary")),
        name="dil_attention",
    )(qd, qd, qd, kd, kd, kd, vd, vd, vd)


def _hy_filter_kernel(z_ref, w1_ref, b1_ref, w2_ref, b2_ref, w3_ref, b3_ref, w4_ref, b4_ref,
                      fq_ref, dl_ref, o_ref, *, tl):
    fq = fq_ref[...]
    h = z_ref[...]
    for w_ref, b_ref in ((w1_ref, b1_ref), (w2_ref, b2_ref), (w3_ref, b3_ref)):
        h = jnp.sin(fq * (jnp.dot(h.astype(BF16), w_ref[...], preferred_element_type=F32) + b_ref[...]))
    h = jnp.dot(h.astype(BF16), w4_ref[...], preferred_element_type=F32) + b4_ref[...]
    t = z_ref[:, 0:1]
    decay = jnp.exp(-t * dl_ref[...])
    o_ref[0] = h[:, :HY_CH] * decay
    n = pl.program_id(0) * tl + lax.broadcasted_iota(jnp.int32, (tl, 1), 0)
    o_ref[1] = jnp.where(n == 0, 0.0, h[:, HY_CH:] * decay)


def hy_filter(zfeat, w1, b1, w2, b2, w3, b3, w4, b4, fq, deltas, tl=512):
    seq = zfeat.shape[0]
    full = lambda a: pl.BlockSpec(a.shape, lambda i: (0,) * a.ndim)
    args = (w1, b1, w2, b2, w3, b3, w4, b4, fq, deltas)
    return pl.pallas_call(
        functools.partial(_hy_filter_kernel, tl=tl),
        out_shape=jax.ShapeDtypeStruct((2, seq, HY_CH), F32),
        grid=(seq // tl,),
        in_specs=[pl.BlockSpec((tl, LANES), lambda i: (i, 0))] + [full(a) for a in args],
        out_specs=pl.BlockSpec((2, tl, HY_CH), lambda i: (0, i, 0)),
        compiler_params=_params(("parallel",)),
        name="hy_filter",
    )(zfeat, *args)


def _hy_pre_kernel(x0_ref, x1_ref, v_ref, p0_ref, p1_ref, pv_ref, n0_ref, n1_ref, nv_ref,
                   w_ref, b_ref, x0o_ref, u_ref, *, tl, blocks_per_seq):
    i = pl.program_id(0)
    first = (i % blocks_per_seq) == 0
    last = (i % blocks_per_seq) == blocks_per_seq - 1
    row = lax.broadcasted_iota(jnp.int32, (tl, 1), 0)

    def conv(c_ref, p_ref, n_ref, part):
        c = c_ref[...]
        sl = slice(part * HY_CH, (part + 1) * HY_CH)
        prev_row = jnp.where(first, 0.0, p_ref[7:8, :])
        next_row = jnp.where(last, 0.0, n_ref[0:1, :])
        down = jnp.where(row == 0, prev_row, pltpu.roll(c, 1, axis=0))
        up = jnp.where(row == tl - 1, next_row, pltpu.roll(c, tl - 1, axis=0))
        return down * w_ref[0:1, sl] + c * w_ref[1:2, sl] + up * w_ref[2:3, sl] + b_ref[:, sl]

    x0o_ref[...] = conv(x0_ref, p0_ref, n0_ref, 0)
    u_ref[...] = conv(x1_ref, p1_ref, n1_ref, 1) * conv(v_ref, pv_ref, nv_ref, 2)


def hy_pre(z, conv_w, conv_b, seq, tl=512):
    m = z.shape[0]
    nblk = m // tl
    r8 = tl // 8
    c0 = COL_HY // HY_CH
    cur = lambda p: pl.BlockSpec((tl, HY_CH), lambda i: (i, c0 + p))
    prv = lambda p: pl.BlockSpec((8, HY_CH), lambda i: (jnp.maximum(i * r8 - 1, 0), c0 + p))
    nxt = lambda p: pl.BlockSpec((8, HY_CH), lambda i: (jnp.minimum((i + 1) * r8, m // 8 - 1), c0 + p))
    out = pl.BlockSpec((tl, HY_CH), lambda i: (i, 0))
    shp = jax.ShapeDtypeStruct((m, HY_CH), F32)
    return pl.pallas_call(
        functools.partial(_hy_pre_kernel, tl=tl, blocks_per_seq=seq // tl),
        out_shape=(shp, shp),
        grid=(nblk,),
        in_specs=[cur(0), cur(1), cur(2), prv(0), prv(1), prv(2), nxt(0), nxt(1), nxt(2),
                  pl.BlockSpec((3, 3 * HY_CH), lambda i: (0, 0)),
                  pl.BlockSpec((1, 3 * HY_CH), lambda i: (0, 0))],
        out_specs=(out, out),
        compiler_params=_params(("parallel",)),
        name="hy_pre",
    )(z, z, z, z, z, z, z, z, z, conv_w, conv_b)


def _buffering(block_bytes):
    return dict(pipeline_mode=pl.Buffered(1)) if block_bytes > 8 * 1024 * 1024 else {}


def _dft_slow_kernel(a_ref, x_ref, o_ref, *, p, ph):
    for q in range(DFT_Q):
        xq = x_ref[pl.ds(q, ph, stride=DFT_Q), :].astype(BF16)
        y = jnp.dot(a_ref[...], xq, preferred_element_type=F32)
        o_ref[0, pl.ds(q, p, stride=DFT_Q), :] = y[:p]
        o_ref[1, pl.ds(q, p, stride=DFT_Q), :] = y[p:]


def dft_slow(a, x, tc):
    bsz, seq, ch = x.shape
    p = a.shape[0] // 2
    n = p * DFT_Q
    return pl.pallas_call(
        functools.partial(_dft_slow_kernel, p=p, ph=p // 2),
        out_shape=jax.ShapeDtypeStruct((bsz, 2, n, ch), F32),
        grid=(bsz, ch // tc),
        in_specs=[pl.BlockSpec(a.shape, lambda b, c: (0, 0)),
                  pl.BlockSpec((None, seq, tc), lambda b, c: (b, 0, c))],
        out_specs=pl.BlockSpec((None, 2, n, tc), lambda b, c: (b, 0, 0, c)),
        compiler_params=_params(("parallel", "parallel")),
        name="dft_slow",
    )(a, x)


def _cmul(ar, ai, br, bi):
    return ar * br - ai * bi, ar * bi + ai * br


def _dft_q(mat_ref, xr, xi):
    y = jnp.dot(mat_ref[...], jnp.concatenate([xr, xi], axis=0).astype(BF16),
                preferred_element_type=F32)
    return y[:DFT_Q], y[DFT_Q:]


SLABS = 4


def _hy_kspec_kernel(g_ref, tw_ref, fwd_ref, o_ref):
    for s in range(SLABS):
        rows = slice(s * DFT_Q, (s + 1) * DFT_Q)
        tr, ti = tw_ref[s, 0], tw_ref[s, 1]
        fr, fi = _dft_q(fwd_ref, *_cmul(g_ref[0, 0, rows, :], g_ref[0, 1, rows, :], tr, ti))
        br, bi = _dft_q(fwd_ref, *_cmul(g_ref[1, 0, rows, :], g_ref[1, 1, rows, :], tr, ti))
        o_ref[0, rows, :] = fr + br
        o_ref[1, rows, :] = fi - bi


def hy_kspec(g, tw, fwd, tc=512):
    n = g.shape[2]
    rows = SLABS * DFT_Q
    return pl.pallas_call(
        _hy_kspec_kernel,
        out_shape=jax.ShapeDtypeStruct((2, n, HY_CH), F32),
        grid=(n // rows, HY_CH // tc),
        in_specs=[pl.BlockSpec((2, 2, rows, tc), lambda kp, c: (0, 0, kp, c)),
                  pl.BlockSpec((SLABS, 2, DFT_Q, 1), lambda kp, c: (kp, 0, 0, 0)),
                  pl.BlockSpec((2 * DFT_Q, 2 * DFT_Q), lambda kp, c: (0, 0))],
        out_specs=pl.BlockSpec((2, rows, tc), lambda kp, c: (0, kp, c)),
        compiler_params=_params(("parallel", "parallel")),
        name="hy_kspec",
    )(g, tw, fwd)


def _hy_mid_kernel(g_ref, tw_ref, fwd_ref, inv_ref, ks_ref, o_ref):
    for s in range(SLABS):
        rows = slice(s * DFT_Q, (s + 1) * DFT_Q)
        tr, ti = tw_ref[s, 0], tw_ref[s, 1]
        xr, xi = _dft_q(fwd_ref, *_cmul(g_ref[0, rows, :], g_ref[1, rows, :], tr, ti))
        yr, yi = _cmul(xr, xi, ks_ref[0, rows, :], ks_ref[1, rows, :])
        wr, wi = _dft_q(inv_ref, yr, yi)
        o_ref[0, rows, :], o_ref[1, rows, :] = _cmul(wr, wi, tr, -ti)


def hy_mid(g, tw, fwd, inv, kspec, tc=512):
    bsz, _, n, _ = g.shape
    rows = SLABS * DFT_Q
    return pl.pallas_call(
        _hy_mid_kernel,
        out_shape=jax.ShapeDtypeStruct(g.shape, F32),
        grid=(bsz, n // rows, HY_CH // tc),
        in_specs=[pl.BlockSpec((None, 2, rows, tc), lambda b, kp, c: (b, 0, kp, c)),
                  pl.BlockSpec((SLABS, 2, DFT_Q, 1), lambda b, kp, c: (kp, 0, 0, 0)),
                  pl.BlockSpec((2 * DFT_Q, 2 * DFT_Q), lambda b, kp, c: (0, 0)),
                  pl.BlockSpec((2 * DFT_Q, 2 * DFT_Q), lambda b, kp, c: (0, 0)),
                  pl.BlockSpec((2, rows, tc), lambda b, kp, c: (0, kp, c))],
        out_specs=pl.BlockSpec((None, 2, rows, tc), lambda b, kp, c: (b, 0, kp, c)),
        compiler_params=_params(("parallel", "parallel", "parallel")),
        name="hy_mid",
    )(g, tw, fwd, inv, kspec)


def _hy_post_kernel(a_ref, h_ref, u_ref, x0_ref, skip_ref, o_ref, *, p, ph, inv_n):
    skip = skip_ref[...]
    for q in range(DFT_Q):
        hq = jnp.concatenate([h_ref[0, pl.ds(q, p, stride=DFT_Q), :],
                              h_ref[1, pl.ds(q, p, stride=DFT_Q), :]], axis=0).astype(BF16)
        y = jnp.dot(a_ref[...], hq, preferred_element_type=F32) * inv_n
        tq = pl.ds(q, ph, stride=DFT_Q)
        o_ref[tq, :] = x0_ref[tq, :] * (y + u_ref[tq, :] * skip)


def hy_post(a, h, u, x0, skip, tc):
    bsz, _, n, ch = h.shape
    seq = n // 2
    p = n // DFT_Q
    seq_spec = pl.BlockSpec((None, seq, tc), lambda b, c: (b, 0, c))
    return pl.pallas_call(
        functools.partial(_hy_post_kernel, p=p, ph=p // 2, inv_n=1.0 / n),
        out_shape=jax.ShapeDtypeStruct((bsz, seq, ch), F32),
        grid=(bsz, ch // tc),
        in_specs=[pl.BlockSpec(a.shape, lambda b, c: (0, 0)),
                  pl.BlockSpec((None, 2, n, tc), lambda b, c: (b, 0, 0, c), **_buffering(2 * n * tc * 4)),
                  seq_spec, seq_spec,
                  pl.BlockSpec((1, tc), lambda b, c: (0, c))],
        out_specs=seq_spec,
        compiler_params=_params(("parallel", "parallel")),
        name="hy_post",
    )(a, h, u, x0, skip)


def _dft_constants(seq):
    n = 2 * seq
    q = DFT_Q
    p = n // q
    ph = p // 2
    fp = np.exp(-2j * np.pi * np.outer(np.arange(p), np.arange(p)) / p)
    fq = np.exp(-2j * np.pi * np.outer(np.arange(q), np.arange(q)) / q)
    tw = np.exp(-2j * np.pi * np.outer(np.arange(p), np.arange(q)) / n)
    a1 = np.concatenate([fp.real[:, :ph], fp.imag[:, :ph]], axis=0)
    fwd = np.block([[fq.real, -fq.imag], [fq.imag, fq.real]])
    inv = np.block([[fq.real, fq.imag], [-fq.imag, fq.real]])
    a3 = np.concatenate([fp.real[:ph, :], fp.imag[:ph, :]], axis=1)
    twa = np.stack([tw.real, tw.imag], axis=1)[..., None]
    cast = lambda a: jnp.asarray(a, F32).astype(BF16)
    return dict(n=n, p=p, ph=ph, a1=cast(a1), fwd=cast(fwd), inv=cast(inv), a3=cast(a3),
                tw=jnp.asarray(twa, F32))


def _hy_features(seq):
    t = jnp.linspace(0.0, 1.0, seq, dtype=F32)[:, None]
    w = 2.0 * math.pi * jnp.arange(seq, dtype=F32)[:, None] / seq
    f = jnp.linspace(1e-4, HY_BANDS - 1, HY_BANDS, dtype=F32)[None, :]
    z = jnp.concatenate([t, jnp.cos(f * w), -jnp.sin(f * w)], axis=-1)
    return jnp.pad(z, ((0, 0), (0, LANES - HY_EMB)))


def _hy_deltas():
    max_decay = math.log(1e-2) / 0.3
    min_decay = math.log(1e-2) / 1.5
    return jnp.abs(jnp.linspace(min_decay, max_decay, HY_CH, dtype=F32))[None, :]


def _pad2(a, rows, cols, value=0.0):
    return jnp.pad(a, ((0, rows - a.shape[0]), (0, cols - a.shape[1])), constant_values=value)


def hyena_spectrum(seq, consts, fw):
    h = hy_filter(_hy_features(seq), *fw, _hy_deltas(), tl=min(512, seq))
    g = dft_slow(consts["a1"], h, _hy_channel_tile(seq))
    return hy_kspec(g, consts["tw"], consts["fwd"])


def _hy_channel_tile(seq):
    del seq
    return LANES


def hyena_conv(x0, u, batch, seq, consts, kspec, skip):
    tc = _hy_channel_tile(seq)
    uv = u.reshape(batch, seq, HY_CH)
    g = dft_slow(consts["a1"], uv, tc)
    hm = hy_mid(g, consts["tw"], consts["fwd"], consts["inv"], kspec)
    y = hy_post(consts["a3"], hm, uv, x0.reshape(batch, seq, HY_CH), skip, tc)
    return y.reshape(batch * seq, HY_CH)


def _layer_params(l, norm_mix, w_in, hy_conv_w, hy_conv_b, hy_f_w1, hy_f_b1, hy_f_w2, hy_f_b2, hy_f_w3,
                  hy_f_b3, hy_f_w4, hy_f_b4, hy_f_freq, hy_skip, mla_q_a_norm, mla_w_q_b, mla_kv_a_norm,
                  mla_w_kv_b, mla_qn_nope, mla_qn_rope, mla_kn_nope, mla_kn_rope, dil_q_norm, dil_k_norm,
                  out_norm, w_out, norm_ffn, w_up, w_down):
    d = w_in.shape[1]
    wi = w_in[l]
    hy_cols = 3 * HY_CH
    o_cq = hy_cols
    o_ckv = o_cq + Q_LORA
    o_kr = o_ckv + KV_LORA
    o_dil = o_kr + ROPE_DIM
    w_cat = jnp.concatenate([
        wi[:, o_dil:], wi[:, o_cq:o_ckv], wi[:, :hy_cols], wi[:, o_ckv:o_kr],
        _spread_halves(wi[:, o_kr:o_dil]), jnp.zeros((d, Z_COLS - Z_USED), wi.dtype)], axis=1).astype(BF16)
    wq = mla_w_q_b[l].reshape(Q_LORA, HEADS, QK_DIM)
    wq = jnp.concatenate([wq[..., :HEAD_DIM], _spread_halves(wq[..., HEAD_DIM:])], axis=-1)
    wide = HEAD_GROUP * 2 * LANES
    wq = wq.reshape(Q_LORA, HEADS // HEAD_GROUP, wide).transpose(1, 0, 2).astype(BF16)
    wkv = mla_w_kv_b[l].reshape(KV_LORA, HEADS // HEAD_GROUP, wide).transpose(1, 0, 2).astype(BF16)
    hf = LANES
    filt = (_pad2(hy_f_w1[l], hf, hf).astype(BF16), _pad2(hy_f_b1[l][None], 1, hf),
            _pad2(hy_f_w2[l], hf, hf).astype(BF16), _pad2(hy_f_b2[l][None], 1, hf),
            _pad2(hy_f_w3[l], hf, hf).astype(BF16), _pad2(hy_f_b3[l][None], 1, hf),
            _pad2(hy_f_w4[l], hf, 2 * HY_CH).astype(BF16), hy_f_b4[l][None],
            _pad2(hy_f_freq[l][None], 1, hf, 1.0))
    return dict(
        norm_mix=norm_mix[l], w_in=w_cat,
        conv_w=hy_conv_w[l], conv_b=hy_conv_b[l][None], filt=filt,
        skip=hy_skip[l][None],
        q_a_norm=mla_q_a_norm[l][None], w_q=wq,
        q_gain=jnp.concatenate([mla_qn_nope[l], _spread_halves(mla_qn_rope[l])])[None],
        kv_a_norm=mla_kv_a_norm[l][None], w_kv=wkv,
        kn_nope=mla_kn_nope[l][None], kn_rope=_spread_halves(mla_kn_rope[l])[None],
        dil_q=dil_q_norm[l][None], dil_k=dil_k_norm[l][None],
        out_norm=out_norm[l], norm_ffn=norm_ffn[l], layer=l)


def mixers(z, p, batch, seq, consts, tabs):
    mla_cos, mla_sin, dil_cos, dil_sin = tabs
    x0, u = hy_pre(z, p["conv_w"], p["conv_b"], seq)
    kspec = hyena_spectrum(seq, consts, p["filt"])
    y_hy = hyena_conv(x0, u, batch, seq, consts, kspec, p["skip"])
    q = mla_q(z, p["q_a_norm"], p["w_q"], p["q_gain"], mla_cos, mla_sin)
    k, v = mla_kv(z, p["kv_a_norm"], p["w_kv"], p["kn_nope"], p["kn_rope"], mla_cos, mla_sin)
    y_mla = mla_attention(q, k, v, batch, seq)
    qd, kd, vd = dil_prep(z, p["dil_q"], p["dil_k"], dil_cos, dil_sin)
    return (y_hy, y_mla) + tuple(dil_attention(qd, kd, vd, batch, seq))


def rope_tabs(batch, seq):
    pos = jnp.tile(jnp.arange(seq, dtype=jnp.int32), batch)
    c64, s64 = _rope_tables(pos, ROPE_DIM)
    c128, s128 = _rope_tables(pos, HEAD_DIM)
    return (_spread_halves(jnp.concatenate([c64, c64], -1)),
            _spread_halves(jnp.concatenate([-s64, s64], -1)),
            jnp.concatenate([c128, c128], -1), jnp.concatenate([-s128, s128], -1))


def _trunk(xin, layers, shared):
    batch, seq, d = xin.shape
    x = xin.reshape(batch * seq, d)
    tabs = rope_tabs(batch, seq)
    consts = _dft_constants(seq)
    for p in layers:
        z = norm_matmul(x, p["norm_mix"], p["w_in"])
        ys = mixers(z, p, batch, seq, consts, tabs)
        x = out_proj(*ys, p["out_norm"], shared["w_out"], x, p["layer"])
        x = ffn(x, p["norm_ffn"], shared["w_up"], shared["w_down"], p["layer"])
    return x.reshape(xin.shape)


def kernel(x_prompt, x_sample, norm_mix, w_in, hy_conv_w, hy_conv_b, hy_f_w1, hy_f_b1, hy_f_w2, hy_f_b2, hy_f_w3, hy_f_b3, hy_f_w4, hy_f_b4, hy_f_freq, hy_skip, mla_q_a_norm, mla_w_q_b, mla_kv_a_norm, mla_w_kv_b, mla_qn_nope, mla_qn_rope, mla_kn_nope, mla_kn_rope, dil_q_norm, dil_k_norm, out_norm, w_out, norm_ffn, w_up, w_down):
    weights = (norm_mix, w_in, hy_conv_w, hy_conv_b, hy_f_w1, hy_f_b1, hy_f_w2, hy_f_b2, hy_f_w3, hy_f_b3,
               hy_f_w4, hy_f_b4, hy_f_freq, hy_skip, mla_q_a_norm, mla_w_q_b, mla_kv_a_norm, mla_w_kv_b,
               mla_qn_nope, mla_qn_rope, mla_kn_nope, mla_kn_rope, dil_q_norm, dil_k_norm, out_norm, w_out,
               norm_ffn, w_up, w_down)
    layers = [_layer_params(l, *weights) for l in range(norm_mix.shape[0])]
    shared = dict(w_out=w_out.astype(BF16), w_up=w_up.astype(BF16), w_down=w_down.astype(BF16))
    return tuple(_trunk(xin, layers, shared) for xin in (x_prompt, x_sample))
```

```python
import functools
import math

import numpy as np
import jax
import jax.numpy as jnp
from jax import lax
from jax.experimental import pallas as pl
from jax.experimental.pallas import tpu as pltpu

F32 = jnp.float32
BF16 = jnp.bfloat16

EPS = 1e-6
NEG = -1e30
ROPE_THETA = 10000.0
LANES = 128
VMEM_LIMIT = 56 * 1024 * 1024

HY_CH = 1024
GROUP_W = 1536
HEADS = 12
HEAD_DIM = 128
ROPE_DIM = 64
QK_DIM = HEAD_DIM + ROPE_DIM
Q_LORA = 1536
KV_LORA = 512
DIL_PAIRS = ((128, 1), (512, 4), (2048, 16))
DIL_SLOTS = 4
HY_EMB = 33
HY_BANDS = 16
HY_FFN = 64
HEAD_GROUP = 4

COL_DQ, COL_DK, COL_DV = 0, 1536, 3072
COL_CQ = 4608
COL_HY = 6144
COL_CKV = 9216
COL_KR = 9728
Z_USED = 9856
Z_COLS = 10240
DFT_Q = 128


def _params(sem, vmem=VMEM_LIMIT):
    return pltpu.CompilerParams(dimension_semantics=sem, vmem_limit_bytes=vmem)


def _rms_scale(x, width):
    return lax.rsqrt(jnp.sum(x * x, axis=-1, keepdims=True) * (1.0 / width) + EPS)


NORM_ROWS = 256


def _norm_matmul_kernel(x_ref, g_ref, w_ref, o_ref, h_ref):
    @pl.when(pl.program_id(1) == 0)
    def _():
        for r in range(0, x_ref.shape[0], NORM_ROWS):
            x = x_ref[r:r + NORM_ROWS, :]
            h_ref[r:r + NORM_ROWS, :] = (x * _rms_scale(x, x.shape[-1]) * g_ref[...]).astype(BF16)

    o_ref[...] = jnp.dot(h_ref[...], w_ref[...], preferred_element_type=F32).astype(o_ref.dtype)


def norm_matmul(x, g, w, tm=1024, tn=1024):
    m, d = x.shape
    n = w.shape[1]
    return pl.pallas_call(
        _norm_matmul_kernel,
        out_shape=jax.ShapeDtypeStruct((m, n), BF16),
        grid=(m // tm, n // tn),
        in_specs=[pl.BlockSpec((tm, d), lambda i, j: (i, 0), pipeline_mode=pl.Buffered(1)),
                  pl.BlockSpec((1, d), lambda i, j: (0, 0)),
                  pl.BlockSpec((d, tn), lambda i, j: (0, j))],
        out_specs=pl.BlockSpec((tm, tn), lambda i, j: (i, j)),
        scratch_shapes=[pltpu.VMEM((tm, d), BF16)],
        compiler_params=_params(("parallel", "arbitrary")),
        name="norm_matmul",
    )(x, g.reshape(1, d), w)


def _out_proj_kernel(hy_ref, mla_ref, d0_ref, d1_ref, d2_ref, g_ref, w_ref, x_ref, o_ref, h_ref):
    @pl.when(pl.program_id(1) == 0)
    def _():
        base = HY_CH + GROUP_W
        gw = GROUP_W // 3
        for r0 in range(0, h_ref.shape[0], NORM_ROWS):
            rs = slice(r0, r0 + NORM_ROWS)
            hy = hy_ref[rs, :]
            h_ref[rs, :HY_CH] = (hy * _rms_scale(hy, HY_CH) * g_ref[:, :HY_CH]).astype(BF16)
            ml = mla_ref[rs, :].astype(F32)
            h_ref[rs, HY_CH:base] = (ml * _rms_scale(ml, GROUP_W) * g_ref[:, HY_CH:base]).astype(BF16)
            d0, d1, d2 = (r_[rs, :].astype(F32) for r_ in (d0_ref, d1_ref, d2_ref))
            ss = (jnp.sum(d0 * d0, axis=-1, keepdims=True) + jnp.sum(d1 * d1, axis=-1, keepdims=True)
                  + jnp.sum(d2 * d2, axis=-1, keepdims=True))
            r = lax.rsqrt(ss * (1.0 / GROUP_W) + EPS)
            for k, dk in enumerate((d0, d1, d2)):
                lo = base + k * gw
                h_ref[rs, lo:lo + gw] = (dk * r * g_ref[:, lo:lo + gw]).astype(BF16)

    o_ref[...] = x_ref[...] + jnp.dot(h_ref[...], w_ref[...], preferred_element_type=F32)


def out_proj(y_hy, y_mla, y_d0, y_d1, y_d2, g, w, x, layer, tm=1024, tn=512):
    m, d = x.shape
    mix = w.shape[1]
    gw = GROUP_W // 3
    row = lambda i, j: (i, 0)
    return pl.pallas_call(
        _out_proj_kernel,
        out_shape=jax.ShapeDtypeStruct((m, d), F32),
        grid=(m // tm, d // tn),
        in_specs=[pl.BlockSpec((tm, HY_CH), row),
                  pl.BlockSpec((tm, GROUP_W), row),
                  pl.BlockSpec((tm, gw), row),
                  pl.BlockSpec((tm, gw), row),
                  pl.BlockSpec((tm, gw), row),
                  pl.BlockSpec((1, mix), lambda i, j: (0, 0)),
                  pl.BlockSpec((None, mix, tn), lambda i, j: (layer, 0, j)),
                  pl.BlockSpec((tm, tn), lambda i, j: (i, j))],
        out_specs=pl.BlockSpec((tm, tn), lambda i, j: (i, j)),
        scratch_shapes=[pltpu.VMEM((tm, mix), BF16)],
        compiler_params=_params(("parallel", "arbitrary")),
        name="out_proj",
    )(y_hy, y_mla, y_d0, y_d1, y_d2, g.reshape(1, mix), w, x)


def _ffn_kernel(x_ref, g_ref, wu_ref, wd_ref, o_ref, h_ref):
    @pl.when(pl.program_id(1) == 0)
    def _():
        x = x_ref[...]
        h_ref[...] = (x * _rms_scale(x, x.shape[-1]) * g_ref[...]).astype(BF16)
        o_ref[...] = x

    a = jnp.maximum(jnp.dot(h_ref[...], wu_ref[...], preferred_element_type=F32), 0.0)
    o_ref[...] += jnp.dot((a * a).astype(BF16), wd_ref[...], preferred_element_type=F32)


def ffn(x, g, w_up, w_down, layer, tm=512, tf=512):
    m, d = x.shape
    f = w_up.shape[2]
    return pl.pallas_call(
        _ffn_kernel,
        out_shape=jax.ShapeDtypeStruct((m, d), F32),
        grid=(m // tm, f // tf),
        in_specs=[pl.BlockSpec((tm, d), lambda i, j: (i, 0), pipeline_mode=pl.Buffered(1)),
                  pl.BlockSpec((1, d), lambda i, j: (0, 0)),
                  pl.BlockSpec((None, d, tf), lambda i, j: (layer, 0, j)),
                  pl.BlockSpec((None, tf, d), lambda i, j: (layer, j, 0))],
        out_specs=pl.BlockSpec((tm, d), lambda i, j: (i, 0)),
        scratch_shapes=[pltpu.VMEM((tm, d), BF16)],
        compiler_params=_params(("parallel", "arbitrary")),
        name="ffn",
    )(x, g.reshape(1, d), w_up, w_down)


def _rope_tables(pos, dim):
    inv = 1.0 / (ROPE_THETA ** (jnp.arange(0, dim, 2, dtype=F32) / dim))
    ang = pos.astype(F32)[:, None] * inv[None, :]
    return jnp.cos(ang), jnp.sin(ang)


def _spread_halves(a):
    z = jnp.zeros(a.shape[:-1] + (ROPE_DIM // 2,), a.dtype)
    return jnp.concatenate([a[..., :ROPE_DIM // 2], z, a[..., ROPE_DIM // 2:], z], axis=-1)


def _rotate_half(x, cos, sin_signed):
    return x * cos + pltpu.roll(x, LANES // 2, axis=1) * sin_signed


def _mla_q_kernel(cq_ref, ga_ref, w_ref, gn_ref, cos_ref, sin_ref, q_ref, c_ref):
    @pl.when(pl.program_id(1) == 0)
    def _():
        c = cq_ref[...].astype(F32)
        c_ref[...] = (c * _rms_scale(c, Q_LORA) * ga_ref[...]).astype(BF16)

    qs = jnp.dot(c_ref[...], w_ref[...], preferred_element_type=F32)
    scale = QK_DIM ** -0.5 * math.log2(math.e)
    for h in range(HEAD_GROUP):
        q = qs[:, h * 2 * LANES:(h + 1) * 2 * LANES]
        qn = q[:, :HEAD_DIM]
        qn = qn * _rms_scale(qn, HEAD_DIM) * gn_ref[:, :HEAD_DIM]
        qr = q[:, HEAD_DIM:]
        qr = qr * _rms_scale(qr, ROPE_DIM) * gn_ref[:, HEAD_DIM:]
        qr = _rotate_half(qr, cos_ref[...], sin_ref[...])
        q_ref[h, :, :HEAD_DIM] = (qn * scale).astype(BF16)
        q_ref[h, :, HEAD_DIM:] = (qr * scale).astype(BF16)


def mla_q(z, ga, w, gn, cos, sin, tm=512):
    m = z.shape[0]
    wide = HEAD_GROUP * 2 * LANES
    return pl.pallas_call(
        _mla_q_kernel,
        out_shape=jax.ShapeDtypeStruct((HEADS, m, 2 * LANES), BF16),
        grid=(m // tm, HEADS // HEAD_GROUP),
        in_specs=[pl.BlockSpec((tm, Q_LORA), lambda i, h: (i, COL_CQ // Q_LORA)),
                  pl.BlockSpec((1, Q_LORA), lambda i, h: (0, 0)),
                  pl.BlockSpec((None, Q_LORA, wide), lambda i, h: (h, 0, 0)),
                  pl.BlockSpec((1, 2 * LANES), lambda i, h: (0, 0)),
                  pl.BlockSpec((tm, LANES), lambda i, h: (i, 0)),
                  pl.BlockSpec((tm, LANES), lambda i, h: (i, 0))],
        out_specs=pl.BlockSpec((HEAD_GROUP, tm, 2 * LANES), lambda i, h: (h, i, 0)),
        scratch_shapes=[pltpu.VMEM((tm, Q_LORA), BF16)],
        compiler_params=_params(("parallel", "arbitrary")),
        name="mla_q",
    )(z, ga, w, gn, cos, sin)


def _mla_kv_kernel(ckv_ref, kr_ref, ga_ref, w_ref, gk_ref, gr_ref, cos_ref, sin_ref,
                   k_ref, v_ref, c_ref, r_ref):
    @pl.when(pl.program_id(1) == 0)
    def _():
        c = ckv_ref[...].astype(F32)
        c_ref[...] = (c * _rms_scale(c, KV_LORA) * ga_ref[...]).astype(BF16)
        kr = kr_ref[...].astype(F32)
        kr = kr * _rms_scale(kr, ROPE_DIM) * gr_ref[...]
        r_ref[...] = _rotate_half(kr, cos_ref[...], sin_ref[...]).astype(BF16)

    kvs = jnp.dot(c_ref[...], w_ref[...], preferred_element_type=F32)
    lane = lax.broadcasted_iota(jnp.int32, (c_ref.shape[0], LANES), 1)
    ones_col = jnp.where(lane == 0, 1.0, 0.0).astype(BF16)
    for h in range(HEAD_GROUP):
        kv = kvs[:, h * 2 * LANES:(h + 1) * 2 * LANES]
        kn = kv[:, :HEAD_DIM]
        k_ref[h, :, :HEAD_DIM] = (kn * _rms_scale(kn, HEAD_DIM) * gk_ref[...]).astype(BF16)
        k_ref[h, :, HEAD_DIM:] = r_ref[...]
        v_ref[h, :, :HEAD_DIM] = kv[:, HEAD_DIM:].astype(BF16)
        v_ref[h, :, HEAD_DIM:] = ones_col


def mla_kv(z, ga, w, gk, gr, cos, sin, tm=512):
    m = z.shape[0]
    wide = HEAD_GROUP * 2 * LANES
    return pl.pallas_call(
        _mla_kv_kernel,
        out_shape=(jax.ShapeDtypeStruct((HEADS, m, 2 * LANES), BF16),
                   jax.ShapeDtypeStruct((HEADS, m, 2 * LANES), BF16)),
        grid=(m // tm, HEADS // HEAD_GROUP),
        in_specs=[pl.BlockSpec((tm, KV_LORA), lambda i, h: (i, COL_CKV // KV_LORA)),
                  pl.BlockSpec((tm, LANES), lambda i, h: (i, COL_KR // LANES)),
                  pl.BlockSpec((1, KV_LORA), lambda i, h: (0, 0)),
                  pl.BlockSpec((None, KV_LORA, wide), lambda i, h: (h, 0, 0)),
                  pl.BlockSpec((1, LANES), lambda i, h: (0, 0)),
                  pl.BlockSpec((1, LANES), lambda i, h: (0, 0)),
                  pl.BlockSpec((tm, LANES), lambda i, h: (i, 0)),
                  pl.BlockSpec((tm, LANES), lambda i, h: (i, 0))],
        out_specs=(pl.BlockSpec((HEAD_GROUP, tm, 2 * LANES), lambda i, h: (h, i, 0)),
                   pl.BlockSpec((HEAD_GROUP, tm, 2 * LANES), lambda i, h: (h, i, 0))),
        scratch_shapes=[pltpu.VMEM((tm, KV_LORA), BF16), pltpu.VMEM((tm, LANES), BF16)],
        compiler_params=_params(("parallel", "arbitrary")),
        name="mla_kv",
    )(z, z, ga, w, gk, gr, cos, sin)


def _flash_kernel(q_ref, k_ref, v_ref, o_ref, m_ref, acc_ref, *, nk, sub):
    j = pl.program_id(3)

    @pl.when(j == 0)
    def _():
        m_ref[...] = jnp.full(m_ref.shape, -jnp.inf, F32)
        acc_ref[...] = jnp.zeros(acc_ref.shape, F32)

    k, v = k_ref[...], v_ref[...]

    def scores(r):
        return lax.dot_general(q_ref[r:r + sub, :], k, (((1,), (1,)), ((), ())),
                               preferred_element_type=F32)

    starts = list(range(0, q_ref.shape[0], sub))
    s_next = scores(starts[0])
    for idx, r in enumerate(starts):
        rows = slice(r, r + sub)
        s = s_next
        if idx + 1 < len(starts):
            s_next = scores(starts[idx + 1])
        m = m_ref[rows, :]
        m_new = jnp.maximum(m, jnp.max(s, axis=-1, keepdims=True))
        alpha = jnp.exp2(m - m_new)
        p = jnp.exp2((s - m_new).astype(BF16))
        acc_ref[rows, :] = alpha * acc_ref[rows, :] + jnp.dot(p, v, preferred_element_type=F32)
        m_ref[rows, :] = m_new

    @pl.when(j == nk - 1)
    def _():
        o_ref[...] = (acc_ref[:, :HEAD_DIM] / acc_ref[:, HEAD_DIM:HEAD_DIM + 1]).astype(o_ref.dtype)


def mla_attention(q, k, v, batch, seq, tq=2048, tk=2048, sub=256):
    tq, tk = min(tq, seq), min(tk, seq)
    sub = min(sub, tq)
    nq, nk = seq // tq, seq // tk
    return pl.pallas_call(
        functools.partial(_flash_kernel, nk=nk, sub=sub),
        out_shape=jax.ShapeDtypeStruct((batch * seq, GROUP_W), BF16),
        grid=(batch, HEADS, nq, nk),
        in_specs=[pl.BlockSpec((None, tq, 2 * LANES), lambda b, h, i, j: (h, b * nq + i, 0)),
                  pl.BlockSpec((None, tk, 2 * LANES), lambda b, h, i, j: (h, b * nk + j, 0)),
                  pl.BlockSpec((None, tk, 2 * LANES), lambda b, h, i, j: (h, b * nk + j, 0))],
        out_specs=pl.BlockSpec((tq, LANES), lambda b, h, i, j: (b * nq + i, h)),
        scratch_shapes=[pltpu.VMEM((tq, 1), F32), pltpu.VMEM((tq, 2 * LANES), F32)],
        compiler_params=_params(("parallel", "parallel", "parallel", "arbitrary")),
        name="mla_attention",
    )(q, k, v)


def _dil_prep_kernel(q_ref, k_ref, gq_ref, gk_ref, cos_ref, sin_ref, qo_ref, ko_ref):
    cos, sin = cos_ref[...], sin_ref[...]
    scale = HEAD_DIM ** -0.5 * math.log2(math.e)
    for h in range(HEADS):
        sl = slice(h * HEAD_DIM, (h + 1) * HEAD_DIM)
        q = q_ref[:, sl].astype(F32)
        q = _rotate_half(q * _rms_scale(q, HEAD_DIM) * gq_ref[...], cos, sin)
        qo_ref[:, sl] = (q * scale).astype(BF16)
        k = k_ref[:, sl].astype(F32)
        k = _rotate_half(k * _rms_scale(k, HEAD_DIM) * gk_ref[...], cos, sin)
        ko_ref[:, sl] = k.astype(BF16)


def dil_prep(z, gq, gk, cos, sin, tm=512):
    m = z.shape[0]
    sec = lambda c: pl.BlockSpec((tm, GROUP_W), lambda i: (i, c // GROUP_W))
    vec = pl.BlockSpec((1, LANES), lambda i: (0, 0))
    tab = pl.BlockSpec((tm, LANES), lambda i: (i, 0))
    out = pl.BlockSpec((tm, GROUP_W), lambda i: (i, 0))
    shp = jax.ShapeDtypeStruct((m, GROUP_W), BF16)
    return pl.pallas_call(
        _dil_prep_kernel,
        out_shape=(shp, shp),
        grid=(m // tm,),
        in_specs=[sec(COL_DQ), sec(COL_DK), vec, vec, tab, tab],
        out_specs=(out, out),
        compiler_params=_params(("parallel",)),
        name="dil_prep",
    )(z, z, gq, gk, cos, sin)


def _dil_windows(seq, tq):
    return [(dil, window // 2, min(seq, tq + window)) for window, dil in DIL_PAIRS]


def _dil_attn_kernel(*refs, seq, tq):
    q_refs, k_refs, v_refs, o_refs = refs[0:3], refs[3:6], refs[6:9], refs[9:12]
    i = pl.program_id(2)
    plan = []
    for g, (dil, reach, win) in enumerate(_dil_windows(seq, tq)):
        start = jnp.clip(i * tq - reach, 0, seq - win)
        start = pl.multiple_of(start, 64)
        s = lax.dot_general(q_refs[g][...], k_refs[g][pl.ds(start, win), :], (((1,), (1,)), ((), ())),
                            preferred_element_type=F32)
        plan.append((g, dil, reach, win, start, s))
    outs, lses = [], []
    for g, dil, reach, win, start, s in plan:
        diff = (lax.broadcasted_iota(jnp.int32, (tq, win), 1)
                - lax.broadcasted_iota(jnp.int32, (tq, win), 0)) + (start - i * tq)
        valid = jnp.abs(diff) <= reach
        if dil > 1:
            valid = valid & ((diff & (dil - 1)) == 0)
        s = jnp.where(valid, s, NEG)
        m = jnp.max(s, axis=-1, keepdims=True)
        p = jnp.exp2(s - m)
        l = jnp.sum(p, axis=-1, keepdims=True)
        v = v_refs[g][pl.ds(start, win), :]
        outs.append(jnp.dot(p.astype(BF16), v, preferred_element_type=F32) / l)
        lses.append(m + jnp.log2(l))
    top = jnp.maximum(jnp.maximum(lses[0], lses[1]), lses[2])
    es = [jnp.exp2(t - top) for t in lses]
    den = es[0] + es[1] + es[2]
    for g in range(3):
        o_refs[g][...] = (outs[g] * (es[g] / den)).astype(o_refs[g].dtype)


def dil_attention(qd, kd, z, batch, seq, tq=256):
    tq = min(tq, seq)
    nq = seq // tq
    v0 = COL_DV // HEAD_DIM
    qs = [pl.BlockSpec((tq, HEAD_DIM), functools.partial(
        lambda b, s, i, g: (b * nq + i, g * DIL_SLOTS + s), g=g)) for g in range(3)]
    ks = [pl.BlockSpec((seq, HEAD_DIM), functools.partial(
        lambda b, s, i, g: (b, g * DIL_SLOTS + s), g=g)) for g in range(3)]
    vs = [pl.BlockSpec((seq, HEAD_DIM), functools.partial(
        lambda b, s, i, g: (b, v0 + g * DIL_SLOTS + s), g=g)) for g in range(3)]
    os_ = [pl.BlockSpec((tq, HEAD_DIM), lambda b, s, i: (b * nq + i, s)) for _ in range(3)]
    shp = jax.ShapeDtypeStruct((batch * seq, DIL_SLOTS * HEAD_DIM), BF16)
    return pl.pallas_call(
        functools.partial(_dil_attn_kernel, seq=seq, tq=tq),
        out_shape=(shp, shp, shp),
        grid=(batch, DIL_SLOTS, nq),
        in_specs=qs + ks + vs,
        out_specs=tuple(os_),
        compiler_params=_params(("parallel", "parallel", "arbitrary")),
        name="dil_attention",
    )(qd, qd, qd, kd, kd, kd, z, z, z)


def _hy_filter_kernel(z_ref, w1_ref, b1_ref, w2_ref, b2_ref, w3_ref, b3_ref, w4_ref, b4_ref,
                      fq_ref, dl_ref, o_ref, *, tl):
    fq = fq_ref[...]
    h = z_ref[...]
    for w_ref, b_ref in ((w1_ref, b1_ref), (w2_ref, b2_ref), (w3_ref, b3_ref)):
        h = jnp.sin(fq * (jnp.dot(h.astype(BF16), w_ref[...], preferred_element_type=F32) + b_ref[...]))
    h = jnp.dot(h.astype(BF16), w4_ref[...], preferred_element_type=F32) + b4_ref[...]
    t = z_ref[:, 0:1]
    decay = jnp.exp(-t * dl_ref[...])
    o_ref[0] = h[:, :HY_CH] * decay
    n = pl.program_id(0) * tl + lax.broadcasted_iota(jnp.int32, (tl, 1), 0)
    o_ref[1] = jnp.where(n == 0, 0.0, h[:, HY_CH:] * decay)


def hy_filter(zfeat, w1, b1, w2, b2, w3, b3, w4, b4, fq, deltas, tl=512):
    seq = zfeat.shape[0]
    full = lambda a: pl.BlockSpec(a.shape, lambda i: (0,) * a.ndim)
    args = (w1, b1, w2, b2, w3, b3, w4, b4, fq, deltas)
    return pl.pallas_call(
        functools.partial(_hy_filter_kernel, tl=tl),
        out_shape=jax.ShapeDtypeStruct((2, seq, HY_CH), F32),
        grid=(seq // tl,),
        in_specs=[pl.BlockSpec((tl, LANES), lambda i: (i, 0))] + [full(a) for a in args],
        out_specs=pl.BlockSpec((2, tl, HY_CH), lambda i: (0, i, 0)),
        compiler_params=_params(("parallel",)),
        name="hy_filter",
    )(zfeat, *args)


HALO = 16


def _hy_pre_kernel(x0_ref, x1_ref, v_ref, p0_ref, p1_ref, pv_ref, n0_ref, n1_ref, nv_ref,
                   w_ref, b_ref, x0o_ref, u_ref, *, tl, blocks_per_seq):
    i = pl.program_id(0)
    first = (i % blocks_per_seq) == 0
    last = (i % blocks_per_seq) == blocks_per_seq - 1
    row = lax.broadcasted_iota(jnp.int32, (tl, 1), 0)

    def conv(c_ref, p_ref, n_ref, part):
        c = c_ref[...].astype(F32)
        sl = slice(part * HY_CH, (part + 1) * HY_CH)
        prev_row = jnp.where(first, 0.0, p_ref[...].astype(F32)[HALO - 1:HALO, :])
        next_row = jnp.where(last, 0.0, n_ref[...].astype(F32)[0:1, :])
        down = jnp.where(row == 0, prev_row, pltpu.roll(c, 1, axis=0))
        up = jnp.where(row == tl - 1, next_row, pltpu.roll(c, tl - 1, axis=0))
        return down * w_ref[0:1, sl] + c * w_ref[1:2, sl] + up * w_ref[2:3, sl] + b_ref[:, sl]

    x0o_ref[...] = conv(x0_ref, p0_ref, n0_ref, 0)
    u_ref[...] = conv(x1_ref, p1_ref, n1_ref, 1) * conv(v_ref, pv_ref, nv_ref, 2)


def hy_pre(z, conv_w, conv_b, seq, tl=512):
    m = z.shape[0]
    nblk = m // tl
    rh = tl // HALO
    c0 = COL_HY // HY_CH
    cur = lambda p: pl.BlockSpec((tl, HY_CH), lambda i: (i, c0 + p))
    prv = lambda p: pl.BlockSpec((HALO, HY_CH), lambda i: (jnp.maximum(i * rh - 1, 0), c0 + p))
    nxt = lambda p: pl.BlockSpec((HALO, HY_CH),
                                 lambda i: (jnp.minimum((i + 1) * rh, m // HALO - 1), c0 + p))
    out = pl.BlockSpec((tl, HY_CH), lambda i: (i, 0))
    shp = jax.ShapeDtypeStruct((m, HY_CH), F32)
    return pl.pallas_call(
        functools.partial(_hy_pre_kernel, tl=tl, blocks_per_seq=seq // tl),
        out_shape=(shp, shp),
        grid=(nblk,),
        in_specs=[cur(0), cur(1), cur(2), prv(0), prv(1), prv(2), nxt(0), nxt(1), nxt(2),
                  pl.BlockSpec((3, 3 * HY_CH), lambda i: (0, 0)),
                  pl.BlockSpec((1, 3 * HY_CH), lambda i: (0, 0))],
        out_specs=(out, out),
        compiler_params=_params(("parallel",)),
        name="hy_pre",
    )(z, z, z, z, z, z, z, z, z, conv_w, conv_b)


def _buffering(block_bytes):
    return dict(pipeline_mode=pl.Buffered(1)) if block_bytes > 8 * 1024 * 1024 else {}


def _dft_slow_kernel(a_ref, x_ref, o_ref, *, p, ph):
    for q in range(DFT_Q):
        xq = x_ref[pl.ds(q, ph, stride=DFT_Q), :].astype(BF16)
        y = jnp.dot(a_ref[...], xq, preferred_element_type=F32)
        o_ref[0, pl.ds(q, p, stride=DFT_Q), :] = y[:p]
        o_ref[1, pl.ds(q, p, stride=DFT_Q), :] = y[p:]


def dft_slow(a, x, tc):
    bsz, seq, ch = x.shape
    p = a.shape[0] // 2
    n = p * DFT_Q
    return pl.pallas_call(
        functools.partial(_dft_slow_kernel, p=p, ph=p // 2),
        out_shape=jax.ShapeDtypeStruct((bsz, 2, n, ch), F32),
        grid=(bsz, ch // tc),
        in_specs=[pl.BlockSpec(a.shape, lambda b, c: (0, 0)),
                  pl.BlockSpec((None, seq, tc), lambda b, c: (b, 0, c))],
        out_specs=pl.BlockSpec((None, 2, n, tc), lambda b, c: (b, 0, 0, c)),
        compiler_params=_params(("parallel", "parallel")),
        name="dft_slow",
    )(a, x)


def _cmul(ar, ai, br, bi):
    return ar * br - ai * bi, ar * bi + ai * br


def _dft_q(mat_ref, xr, xi):
    y = jnp.dot(mat_ref[...], jnp.concatenate([xr, xi], axis=0).astype(BF16),
                preferred_element_type=F32)
    return y[:DFT_Q], y[DFT_Q:]


SLABS = 4


def _hy_kspec_kernel(g_ref, tw_ref, fwd_ref, o_ref):
    for s in range(SLABS):
        rows = slice(s * DFT_Q, (s + 1) * DFT_Q)
        tr, ti = tw_ref[s, 0], tw_ref[s, 1]
        fr, fi = _dft_q(fwd_ref, *_cmul(g_ref[0, 0, rows, :], g_ref[0, 1, rows, :], tr, ti))
        br, bi = _dft_q(fwd_ref, *_cmul(g_ref[1, 0, rows, :], g_ref[1, 1, rows, :], tr, ti))
        o_ref[0, rows, :] = fr + br
        o_ref[1, rows, :] = fi - bi


def hy_kspec(g, tw, fwd, tc=512):
    n = g.shape[2]
    rows = SLABS * DFT_Q
    return pl.pallas_call(
        _hy_kspec_kernel,
        out_shape=jax.ShapeDtypeStruct((2, n, HY_CH), F32),
        grid=(n // rows, HY_CH // tc),
        in_specs=[pl.BlockSpec((2, 2, rows, tc), lambda kp, c: (0, 0, kp, c)),
                  pl.BlockSpec((SLABS, 2, DFT_Q, 1), lambda kp, c: (kp, 0, 0, 0)),
                  pl.BlockSpec((2 * DFT_Q, 2 * DFT_Q), lambda kp, c: (0, 0))],
        out_specs=pl.BlockSpec((2, rows, tc), lambda kp, c: (0, kp, c)),
        compiler_params=_params(("parallel", "parallel")),
        name="hy_kspec",
    )(g, tw, fwd)


def _hy_mid_kernel(g_ref, tw_ref, fwd_ref, inv_ref, ks_ref, o_ref):
    for s in range(SLABS):
        rows = slice(s * DFT_Q, (s + 1) * DFT_Q)
        tr, ti = tw_ref[s, 0], tw_ref[s, 1]
        xr, xi = _dft_q(fwd_ref, *_cmul(g_ref[0, rows, :], g_ref[1, rows, :], tr, ti))
        yr, yi = _cmul(xr, xi, ks_ref[0, rows, :], ks_ref[1, rows, :])
        wr, wi = _dft_q(inv_ref, yr, yi)
        o_ref[0, rows, :], o_ref[1, rows, :] = _cmul(wr, wi, tr, -ti)


def hy_mid(g, tw, fwd, inv, kspec, tc=512):
    bsz, _, n, _ = g.shape
    rows = SLABS * DFT_Q
    return pl.pallas_call(
        _hy_mid_kernel,
        out_shape=jax.ShapeDtypeStruct(g.shape, F32),
        grid=(n // rows, HY_CH // tc, bsz),
        in_specs=[pl.BlockSpec((None, 2, rows, tc), lambda kp, c, b: (b, 0, kp, c)),
                  pl.BlockSpec((SLABS, 2, DFT_Q, 1), lambda kp, c, b: (kp, 0, 0, 0)),
                  pl.BlockSpec((2 * DFT_Q, 2 * DFT_Q), lambda kp, c, b: (0, 0)),
                  pl.BlockSpec((2 * DFT_Q, 2 * DFT_Q), lambda kp, c, b: (0, 0)),
                  pl.BlockSpec((2, rows, tc), lambda kp, c, b: (0, kp, c))],
        out_specs=pl.BlockSpec((None, 2, rows, tc), lambda kp, c, b: (b, 0, kp, c)),
        compiler_params=_params(("parallel", "parallel", "parallel")),
        name="hy_mid",
    )(g, tw, fwd, inv, kspec)


def _hy_post_kernel(a_ref, h_ref, u_ref, x0_ref, skip_ref, o_ref, *, p, ph, inv_n):
    skip = skip_ref[...]
    for q in range(DFT_Q):
        hq = jnp.concatenate([h_ref[0, pl.ds(q, p, stride=DFT_Q), :],
                              h_ref[1, pl.ds(q, p, stride=DFT_Q), :]], axis=0).astype(BF16)
        y = jnp.dot(a_ref[...], hq, preferred_element_type=F32) * inv_n
        tq = pl.ds(q, ph, stride=DFT_Q)
        o_ref[tq, :] = x0_ref[tq, :] * (y + u_ref[tq, :] * skip)


def hy_post(a, h, u, x0, skip, tc):
    bsz, _, n, ch = h.shape
    seq = n // 2
    p = n // DFT_Q
    seq_spec = pl.BlockSpec((None, seq, tc), lambda b, c: (b, 0, c))
    return pl.pallas_call(
        functools.partial(_hy_post_kernel, p=p, ph=p // 2, inv_n=1.0 / n),
        out_shape=jax.ShapeDtypeStruct((bsz, seq, ch), F32),
        grid=(bsz, ch // tc),
        in_specs=[pl.BlockSpec(a.shape, lambda b, c: (0, 0)),
                  pl.BlockSpec((None, 2, n, tc), lambda b, c: (b, 0, 0, c), **_buffering(2 * n * tc * 4)),
                  seq_spec, seq_spec,
                  pl.BlockSpec((1, tc), lambda b, c: (0, c))],
        out_specs=seq_spec,
        compiler_params=_params(("parallel", "parallel")),
        name="hy_post",
    )(a, h, u, x0, skip)


def _dft_constants(seq):
    n = 2 * seq
    q = DFT_Q
    p = n // q
    ph = p // 2
    fp = np.exp(-2j * np.pi * np.outer(np.arange(p), np.arange(p)) / p)
    fq = np.exp(-2j * np.pi * np.outer(np.arange(q), np.arange(q)) / q)
    tw = np.exp(-2j * np.pi * np.outer(np.arange(p), np.arange(q)) / n)
    a1 = np.concatenate([fp.real[:, :ph], fp.imag[:, :ph]], axis=0)
    fwd = np.block([[fq.real, -fq.imag], [fq.imag, fq.real]])
    inv = np.block([[fq.real, fq.imag], [-fq.imag, fq.real]])
    a3 = np.concatenate([fp.real[:ph, :], fp.imag[:ph, :]], axis=1)
    twa = np.stack([tw.real, tw.imag], axis=1)[..., None]
    cast = lambda a: jnp.asarray(a, F32).astype(BF16)
    return dict(n=n, p=p, ph=ph, a1=cast(a1), fwd=cast(fwd), inv=cast(inv), a3=cast(a3),
                tw=jnp.asarray(twa, F32))


def _hy_features(seq):
    t = jnp.linspace(0.0, 1.0, seq, dtype=F32)[:, None]
    w = 2.0 * math.pi * jnp.arange(seq, dtype=F32)[:, None] / seq
    f = jnp.linspace(1e-4, HY_BANDS - 1, HY_BANDS, dtype=F32)[None, :]
    z = jnp.concatenate([t, jnp.cos(f * w), -jnp.sin(f * w)], axis=-1)
    return jnp.pad(z, ((0, 0), (0, LANES - HY_EMB)))


def _hy_deltas():
    max_decay = math.log(1e-2) / 0.3
    min_decay = math.log(1e-2) / 1.5
    return jnp.abs(jnp.linspace(min_decay, max_decay, HY_CH, dtype=F32))[None, :]


def _pad2(a, rows, cols, value=0.0):
    return jnp.pad(a, ((0, rows - a.shape[0]), (0, cols - a.shape[1])), constant_values=value)


def hyena_spectrum(seq, consts, fw):
    h = hy_filter(_hy_features(seq), *fw, _hy_deltas(), tl=min(512, seq))
    g = dft_slow(consts["a1"], h, _hy_channel_tile(seq))
    return hy_kspec(g, consts["tw"], consts["fwd"])


def _hy_channel_tile(seq):
    del seq
    return LANES


def hyena_conv(x0, u, batch, seq, consts, kspec, skip):
    tc = _hy_channel_tile(seq)
    uv = u.reshape(batch, seq, HY_CH)
    g = dft_slow(consts["a1"], uv, tc)
    hm = hy_mid(g, consts["tw"], consts["fwd"], consts["inv"], kspec)
    y = hy_post(consts["a3"], hm, uv, x0.reshape(batch, seq, HY_CH), skip, tc)
    return y.reshape(batch * seq, HY_CH)


def _layer_params(l, norm_mix, w_in, hy_conv_w, hy_conv_b, hy_f_w1, hy_f_b1, hy_f_w2, hy_f_b2, hy_f_w3,
                  hy_f_b3, hy_f_w4, hy_f_b4, hy_f_freq, hy_skip, mla_q_a_norm, mla_w_q_b, mla_kv_a_norm,
                  mla_w_kv_b, mla_qn_nope, mla_qn_rope, mla_kn_nope, mla_kn_rope, dil_q_norm, dil_k_norm,
                  out_norm, w_out, norm_ffn, w_up, w_down):
    d = w_in.shape[1]
    wi = w_in[l]
    hy_cols = 3 * HY_CH
    o_cq = hy_cols
    o_ckv = o_cq + Q_LORA
    o_kr = o_ckv + KV_LORA
    o_dil = o_kr + ROPE_DIM
    w_cat = jnp.concatenate([
        wi[:, o_dil:], wi[:, o_cq:o_ckv], wi[:, :hy_cols], wi[:, o_ckv:o_kr],
        _spread_halves(wi[:, o_kr:o_dil]), jnp.zeros((d, Z_COLS - Z_USED), wi.dtype)], axis=1).astype(BF16)
    wq = mla_w_q_b[l].reshape(Q_LORA, HEADS, QK_DIM)
    wq = jnp.concatenate([wq[..., :HEAD_DIM], _spread_halves(wq[..., HEAD_DIM:])], axis=-1)
    wide = HEAD_GROUP * 2 * LANES
    wq = wq.reshape(Q_LORA, HEADS // HEAD_GROUP, wide).transpose(1, 0, 2).astype(BF16)
    wkv = mla_w_kv_b[l].reshape(KV_LORA, HEADS // HEAD_GROUP, wide).transpose(1, 0, 2).astype(BF16)
    hf = LANES
    filt = (_pad2(hy_f_w1[l], hf, hf).astype(BF16), _pad2(hy_f_b1[l][None], 1, hf),
            _pad2(hy_f_w2[l], hf, hf).astype(BF16), _pad2(hy_f_b2[l][None], 1, hf),
            _pad2(hy_f_w3[l], hf, hf).astype(BF16), _pad2(hy_f_b3[l][None], 1, hf),
            _pad2(hy_f_w4[l], hf, 2 * HY_CH).astype(BF16), hy_f_b4[l][None],
            _pad2(hy_f_freq[l][None], 1, hf, 1.0))
    return dict(
        norm_mix=norm_mix[l], w_in=w_cat,
        conv_w=hy_conv_w[l], conv_b=hy_conv_b[l][None], filt=filt,
        skip=hy_skip[l][None],
        q_a_norm=mla_q_a_norm[l][None], w_q=wq,
        q_gain=jnp.concatenate([mla_qn_nope[l], _spread_halves(mla_qn_rope[l])])[None],
        kv_a_norm=mla_kv_a_norm[l][None], w_kv=wkv,
        kn_nope=mla_kn_nope[l][None], kn_rope=_spread_halves(mla_kn_rope[l])[None],
        dil_q=dil_q_norm[l][None], dil_k=dil_k_norm[l][None],
        out_norm=out_norm[l], norm_ffn=norm_ffn[l], layer=l)


def mixers(z, p, batch, seq, consts, tabs):
    mla_cos, mla_sin, dil_cos, dil_sin = tabs
    x0, u = hy_pre(z, p["conv_w"], p["conv_b"], seq)
    kspec = hyena_spectrum(seq, consts, p["filt"])
    y_hy = hyena_conv(x0, u, batch, seq, consts, kspec, p["skip"])
    q = mla_q(z, p["q_a_norm"], p["w_q"], p["q_gain"], mla_cos, mla_sin)
    k, v = mla_kv(z, p["kv_a_norm"], p["w_kv"], p["kn_nope"], p["kn_rope"], mla_cos, mla_sin)
    y_mla = mla_attention(q, k, v, batch, seq)
    qd, kd = dil_prep(z, p["dil_q"], p["dil_k"], dil_cos, dil_sin)
    return (y_hy, y_mla) + tuple(dil_attention(qd, kd, z, batch, seq))


def rope_tabs(batch, seq):
    pos = jnp.tile(jnp.arange(seq, dtype=jnp.int32), batch)
    c64, s64 = _rope_tables(pos, ROPE_DIM)
    c128, s128 = _rope_tables(pos, HEAD_DIM)
    return (_spread_halves(jnp.concatenate([c64, c64], -1)),
            _spread_halves(jnp.concatenate([-s64, s64], -1)),
            jnp.concatenate([c128, c128], -1), jnp.concatenate([-s128, s128], -1))


def _trunk(xin, layers, shared):
    batch, seq, d = xin.shape
    x = xin.reshape(batch * seq, d)
    tabs = rope_tabs(batch, seq)
    consts = _dft_constants(seq)
    for p in layers:
        z = norm_matmul(x, p["norm_mix"], p["w_in"])
        ys = mixers(z, p, batch, seq, consts, tabs)
        x = out_proj(*ys, p["out_norm"], shared["w_out"], x, p["layer"])
        x = ffn(x, p["norm_ffn"], shared["w_up"], shared["w_down"], p["layer"])
    return x.reshape(xin.shape)


def kernel(x_prompt, x_sample, norm_mix, w_in, hy_conv_w, hy_conv_b, hy_f_w1, hy_f_b1, hy_f_w2, hy_f_b2, hy_f_w3, hy_f_b3, hy_f_w4, hy_f_b4, hy_f_freq, hy_skip, mla_q_a_norm, mla_w_q_b, mla_kv_a_norm, mla_w_kv_b, mla_qn_nope, mla_qn_rope, mla_kn_nope, mla_kn_rope, dil_q_norm, dil_k_norm, out_norm, w_out, norm_ffn, w_up, w_down):
    weights = (norm_mix, w_in, hy_conv_w, hy_conv_b, hy_f_w1, hy_f_b1, hy_f_w2, hy_f_b2, hy_f_w3, hy_f_b3,
               hy_f_w4, hy_f_b4, hy_f_freq, hy_skip, mla_q_a_norm, mla_w_q_b, mla_kv_a_norm, mla_w_kv_b,
               mla_qn_nope, mla_qn_rope, mla_kn_nope, mla_kn_rope, dil_q_norm, dil_k_norm, out_norm, w_out,
               norm_ffn, w_up, w_down)
    layers = [_layer_params(l, *weights) for l in range(norm_mix.shape[0])]
    shared = dict(w_out=w_out.astype(BF16), w_up=w_up.astype(BF16), w_down=w_down.astype(BF16))
    return tuple(_trunk(xin, layers, shared) for xin in (x_prompt, x_sample))
```

```python
import functools
import math

import numpy as np
import jax
import jax.numpy as jnp
from jax import lax
from jax.experimental import pallas as pl
from jax.experimental.pallas import tpu as pltpu

F32 = jnp.float32
BF16 = jnp.bfloat16

EPS = 1e-6
NEG = -1e30
ROPE_THETA = 10000.0
LANES = 128
VMEM_LIMIT = 56 * 1024 * 1024

HY_CH = 1024
GROUP_W = 1536
HEADS = 12
HEAD_DIM = 128
ROPE_DIM = 64
QK_DIM = HEAD_DIM + ROPE_DIM
Q_LORA = 1536
KV_LORA = 512
DIL_PAIRS = ((128, 1), (512, 4), (2048, 16))
DIL_SLOTS = 4
HY_EMB = 33
HY_BANDS = 16
HY_SEQ_TILE = LANES
HEAD_GROUP = 4

COL_DQ, COL_DK, COL_DV = 0, 1536, 3072
COL_CQ = 4608
COL_HY = 6144
COL_CKV = 9216
COL_KR = 9728
Z_USED = 9856
Z_COLS = 10240
DFT_Q = 128


def _params(sem, vmem=VMEM_LIMIT):
    return pltpu.CompilerParams(dimension_semantics=sem, vmem_limit_bytes=vmem)


def _rms_scale(x, width):
    return lax.rsqrt(jnp.sum(x * x, axis=-1, keepdims=True) * (1.0 / width) + EPS)


NORM_ROWS = 256


def _norm_matmul_kernel(x_ref, g_ref, w_ref, o_ref, h_ref):
    @pl.when(pl.program_id(1) == 0)
    def _():
        for r in range(0, x_ref.shape[0], NORM_ROWS):
            x = x_ref[r:r + NORM_ROWS, :]
            h_ref[r:r + NORM_ROWS, :] = (x * _rms_scale(x, x.shape[-1]) * g_ref[...]).astype(BF16)

    o_ref[...] = jnp.dot(h_ref[...], w_ref[...], preferred_element_type=F32).astype(o_ref.dtype)


def norm_matmul(x, g, w, layer, tm=1024, tn=1024):
    m, d = x.shape
    n = w.shape[2]
    return pl.pallas_call(
        _norm_matmul_kernel,
        out_shape=jax.ShapeDtypeStruct((m, n), BF16),
        grid=(m // tm, n // tn),
        in_specs=[pl.BlockSpec((tm, d), lambda i, j: (i, 0), pipeline_mode=pl.Buffered(1)),
                  pl.BlockSpec((1, d), lambda i, j: (0, 0)),
                  pl.BlockSpec((None, d, tn), lambda i, j: (layer, 0, j))],
        out_specs=pl.BlockSpec((tm, tn), lambda i, j: (i, j)),
        scratch_shapes=[pltpu.VMEM((tm, d), BF16)],
        compiler_params=_params(("parallel", "arbitrary")),
        name="norm_matmul",
    )(x, g.reshape(1, d), w)


def _out_proj_kernel(hy_ref, mla_ref, d0_ref, d1_ref, d2_ref, g_ref, w_ref, x_ref, o_ref, h_ref):
    @pl.when(pl.program_id(1) == 0)
    def _():
        base = HY_CH + GROUP_W
        gw = GROUP_W // 3
        for r0 in range(0, h_ref.shape[0], NORM_ROWS):
            rs = slice(r0, r0 + NORM_ROWS)
            hy = hy_ref[rs, :]
            h_ref[rs, :HY_CH] = (hy * _rms_scale(hy, HY_CH) * g_ref[:, :HY_CH]).astype(BF16)
            ml = mla_ref[rs, :].astype(F32)
            h_ref[rs, HY_CH:base] = (ml * _rms_scale(ml, GROUP_W) * g_ref[:, HY_CH:base]).astype(BF16)
            d0, d1, d2 = (r_[rs, :].astype(F32) for r_ in (d0_ref, d1_ref, d2_ref))
            ss = (jnp.sum(d0 * d0, axis=-1, keepdims=True) + jnp.sum(d1 * d1, axis=-1, keepdims=True)
                  + jnp.sum(d2 * d2, axis=-1, keepdims=True))
            r = lax.rsqrt(ss * (1.0 / GROUP_W) + EPS)
            for k, dk in enumerate((d0, d1, d2)):
                lo = base + k * gw
                h_ref[rs, lo:lo + gw] = (dk * r * g_ref[:, lo:lo + gw]).astype(BF16)

    o_ref[...] = x_ref[...] + jnp.dot(h_ref[...], w_ref[...], preferred_element_type=F32)


def out_proj(y_hy, y_mla, y_d0, y_d1, y_d2, g, w, x, layer, tm=1024, tn=512):
    m, d = x.shape
    mix = w.shape[1]
    gw = GROUP_W // 3
    row = lambda i, j: (i, 0)
    return pl.pallas_call(
        _out_proj_kernel,
        out_shape=jax.ShapeDtypeStruct((m, d), F32),
        grid=(m // tm, d // tn),
        in_specs=[pl.BlockSpec((tm, HY_CH), row),
                  pl.BlockSpec((tm, GROUP_W), row),
                  pl.BlockSpec((tm, gw), row),
                  pl.BlockSpec((tm, gw), row),
                  pl.BlockSpec((tm, gw), row),
                  pl.BlockSpec((1, mix), lambda i, j: (0, 0)),
                  pl.BlockSpec((None, mix, tn), lambda i, j: (layer, 0, j)),
                  pl.BlockSpec((tm, tn), lambda i, j: (i, j))],
        out_specs=pl.BlockSpec((tm, tn), lambda i, j: (i, j)),
        scratch_shapes=[pltpu.VMEM((tm, mix), BF16)],
        compiler_params=_params(("parallel", "arbitrary")),
        name="out_proj",
    )(y_hy, y_mla, y_d0, y_d1, y_d2, g.reshape(1, mix), w, x)


def _ffn_kernel(x_ref, g_ref, wu_ref, wd_ref, o_ref, h_ref):
    @pl.when(pl.program_id(1) == 0)
    def _():
        x = x_ref[...]
        h_ref[...] = (x * _rms_scale(x, x.shape[-1]) * g_ref[...]).astype(BF16)
        o_ref[...] = x

    a = jnp.maximum(jnp.dot(h_ref[...], wu_ref[...], preferred_element_type=F32), 0.0)
    o_ref[...] += jnp.dot((a * a).astype(BF16), wd_ref[...], preferred_element_type=F32)


def ffn(x, g, w_up, w_down, layer, tm=512, tf=512):
    m, d = x.shape
    f = w_up.shape[2]
    return pl.pallas_call(
        _ffn_kernel,
        out_shape=jax.ShapeDtypeStruct((m, d), F32),
        grid=(m // tm, f // tf),
        in_specs=[pl.BlockSpec((tm, d), lambda i, j: (i, 0), pipeline_mode=pl.Buffered(1)),
                  pl.BlockSpec((1, d), lambda i, j: (0, 0)),
                  pl.BlockSpec((None, d, tf), lambda i, j: (layer, 0, j)),
                  pl.BlockSpec((None, tf, d), lambda i, j: (layer, j, 0))],
        out_specs=pl.BlockSpec((tm, d), lambda i, j: (i, 0)),
        scratch_shapes=[pltpu.VMEM((tm, d), BF16)],
        compiler_params=_params(("parallel", "arbitrary")),
        name="ffn",
    )(x, g.reshape(1, d), w_up, w_down)


def _rope_tables(pos, dim):
    inv = 1.0 / (ROPE_THETA ** (jnp.arange(0, dim, 2, dtype=F32) / dim))
    ang = pos.astype(F32)[:, None] * inv[None, :]
    return jnp.cos(ang), jnp.sin(ang)


def _spread_halves(a):
    z = jnp.zeros(a.shape[:-1] + (ROPE_DIM // 2,), a.dtype)
    return jnp.concatenate([a[..., :ROPE_DIM // 2], z, a[..., ROPE_DIM // 2:], z], axis=-1)


def _rotate_half(x, cos, sin_signed):
    return x * cos + pltpu.roll(x, LANES // 2, axis=1) * sin_signed


def _mla_q_kernel(cq_ref, ga_ref, w_ref, gn_ref, cos_ref, sin_ref, q_ref, c_ref):
    @pl.when(pl.program_id(1) == 0)
    def _():
        c = cq_ref[...].astype(F32)
        c_ref[...] = (c * _rms_scale(c, Q_LORA) * ga_ref[...]).astype(BF16)

    qs = jnp.dot(c_ref[...], w_ref[...], preferred_element_type=F32)
    scale = QK_DIM ** -0.5 * math.log2(math.e)
    for h in range(HEAD_GROUP):
        q = qs[:, h * 2 * LANES:(h + 1) * 2 * LANES]
        qn = q[:, :HEAD_DIM]
        qn = qn * _rms_scale(qn, HEAD_DIM) * gn_ref[:, :HEAD_DIM]
        qr = q[:, HEAD_DIM:]
        qr = qr * _rms_scale(qr, ROPE_DIM) * gn_ref[:, HEAD_DIM:]
        qr = _rotate_half(qr, cos_ref[...], sin_ref[...])
        q_ref[h, :, :HEAD_DIM] = (qn * scale).astype(BF16)
        q_ref[h, :, HEAD_DIM:] = (qr * scale).astype(BF16)


def mla_q(z, ga, w, gn, cos, sin, tm=512):
    m = z.shape[0]
    wide = HEAD_GROUP * 2 * LANES
    return pl.pallas_call(
        _mla_q_kernel,
        out_shape=jax.ShapeDtypeStruct((HEADS, m, 2 * LANES), BF16),
        grid=(m // tm, HEADS // HEAD_GROUP),
        in_specs=[pl.BlockSpec((tm, Q_LORA), lambda i, h: (i, COL_CQ // Q_LORA)),
                  pl.BlockSpec((1, Q_LORA), lambda i, h: (0, 0)),
                  pl.BlockSpec((None, Q_LORA, wide), lambda i, h: (h, 0, 0)),
                  pl.BlockSpec((1, 2 * LANES), lambda i, h: (0, 0)),
                  pl.BlockSpec((tm, LANES), lambda i, h: (i, 0)),
                  pl.BlockSpec((tm, LANES), lambda i, h: (i, 0))],
        out_specs=pl.BlockSpec((HEAD_GROUP, tm, 2 * LANES), lambda i, h: (h, i, 0)),
        scratch_shapes=[pltpu.VMEM((tm, Q_LORA), BF16)],
        compiler_params=_params(("parallel", "arbitrary")),
        name="mla_q",
    )(z, ga, w, gn, cos, sin)


def _mla_kv_kernel(ckv_ref, kr_ref, ga_ref, w_ref, gk_ref, gr_ref, cos_ref, sin_ref,
                   k_ref, v_ref, c_ref, r_ref):
    @pl.when(pl.program_id(1) == 0)
    def _():
        c = ckv_ref[...].astype(F32)
        c_ref[...] = (c * _rms_scale(c, KV_LORA) * ga_ref[...]).astype(BF16)
        kr = kr_ref[...].astype(F32)
        kr = kr * _rms_scale(kr, ROPE_DIM) * gr_ref[...]
        r_ref[...] = _rotate_half(kr, cos_ref[...], sin_ref[...]).astype(BF16)

    kvs = jnp.dot(c_ref[...], w_ref[...], preferred_element_type=F32)
    lane = lax.broadcasted_iota(jnp.int32, (c_ref.shape[0], LANES), 1)
    ones_col = jnp.where(lane == 0, 1.0, 0.0).astype(BF16)
    for h in range(HEAD_GROUP):
        kv = kvs[:, h * 2 * LANES:(h + 1) * 2 * LANES]
        kn = kv[:, :HEAD_DIM]
        k_ref[h, :, :HEAD_DIM] = (kn * _rms_scale(kn, HEAD_DIM) * gk_ref[...]).astype(BF16)
        k_ref[h, :, HEAD_DIM:] = r_ref[...]
        v_ref[h, :, :HEAD_DIM] = kv[:, HEAD_DIM:].astype(BF16)
        v_ref[h, :, HEAD_DIM:] = ones_col


def mla_kv(z, ga, w, gk, gr, cos, sin, tm=512):
    m = z.shape[0]
    wide = HEAD_GROUP * 2 * LANES
    return pl.pallas_call(
        _mla_kv_kernel,
        out_shape=(jax.ShapeDtypeStruct((HEADS, m, 2 * LANES), BF16),
                   jax.ShapeDtypeStruct((HEADS, m, 2 * LANES), BF16)),
        grid=(m // tm, HEADS // HEAD_GROUP),
        in_specs=[pl.BlockSpec((tm, KV_LORA), lambda i, h: (i, COL_CKV // KV_LORA)),
                  pl.BlockSpec((tm, LANES), lambda i, h: (i, COL_KR // LANES)),
                  pl.BlockSpec((1, KV_LORA), lambda i, h: (0, 0)),
                  pl.BlockSpec((None, KV_LORA, wide), lambda i, h: (h, 0, 0)),
                  pl.BlockSpec((1, LANES), lambda i, h: (0, 0)),
                  pl.BlockSpec((1, LANES), lambda i, h: (0, 0)),
                  pl.BlockSpec((tm, LANES), lambda i, h: (i, 0)),
                  pl.BlockSpec((tm, LANES), lambda i, h: (i, 0))],
        out_specs=(pl.BlockSpec((HEAD_GROUP, tm, 2 * LANES), lambda i, h: (h, i, 0)),
                   pl.BlockSpec((HEAD_GROUP, tm, 2 * LANES), lambda i, h: (h, i, 0))),
        scratch_shapes=[pltpu.VMEM((tm, KV_LORA), BF16), pltpu.VMEM((tm, LANES), BF16)],
        compiler_params=_params(("parallel", "arbitrary")),
        name="mla_kv",
    )(z, z, ga, w, gk, gr, cos, sin)


def _flash_kernel(q_ref, k_ref, v_ref, o_ref, m_ref, acc_ref, *, nk, sub):
    j = pl.program_id(3)

    @pl.when(j == 0)
    def _():
        m_ref[...] = jnp.full(m_ref.shape, -jnp.inf, F32)
        acc_ref[...] = jnp.zeros(acc_ref.shape, F32)

    k, v = k_ref[...], v_ref[...]

    def scores(r):
        return lax.dot_general(q_ref[r:r + sub, :], k, (((1,), (1,)), ((), ())),
                               preferred_element_type=F32)

    starts = list(range(0, q_ref.shape[0], sub))
    s_next = scores(starts[0])
    for idx, r in enumerate(starts):
        rows = slice(r, r + sub)
        s = s_next
        if idx + 1 < len(starts):
            s_next = scores(starts[idx + 1])
        m = m_ref[rows, :]
        m_new = jnp.maximum(m, jnp.max(s, axis=-1, keepdims=True))
        alpha = jnp.exp2(m - m_new)
        p = jnp.exp2((s - m_new).astype(BF16))
        acc_ref[rows, :] = alpha * acc_ref[rows, :] + jnp.dot(p, v, preferred_element_type=F32)
        m_ref[rows, :] = m_new

    @pl.when(j == nk - 1)
    def _():
        o_ref[...] = (acc_ref[:, :HEAD_DIM] / acc_ref[:, HEAD_DIM:HEAD_DIM + 1]).astype(o_ref.dtype)


def mla_attention(q, k, v, batch, seq, tq=4096, tk=2048, sub=256):
    tq, tk = min(tq, seq), min(tk, seq)
    sub = min(sub, tq)
    nq, nk = seq // tq, seq // tk
    return pl.pallas_call(
        functools.partial(_flash_kernel, nk=nk, sub=sub),
        out_shape=jax.ShapeDtypeStruct((batch * seq, GROUP_W), BF16),
        grid=(batch, HEADS, nq, nk),
        in_specs=[pl.BlockSpec((None, tq, 2 * LANES), lambda b, h, i, j: (h, b * nq + i, 0)),
                  pl.BlockSpec((None, tk, 2 * LANES), lambda b, h, i, j: (h, b * nk + j, 0)),
                  pl.BlockSpec((None, tk, 2 * LANES), lambda b, h, i, j: (h, b * nk + j, 0))],
        out_specs=pl.BlockSpec((tq, LANES), lambda b, h, i, j: (b * nq + i, h)),
        scratch_shapes=[pltpu.VMEM((tq, 1), F32), pltpu.VMEM((tq, 2 * LANES), F32)],
        compiler_params=_params(("parallel", "parallel", "parallel", "arbitrary")),
        name="mla_attention",
    )(q, k, v)


def _dil_prep_kernel(q_ref, k_ref, gq_ref, gk_ref, cos_ref, sin_ref, qo_ref, ko_ref):
    cos, sin = cos_ref[...], sin_ref[...]
    scale = HEAD_DIM ** -0.5 * math.log2(math.e)
    for h in range(HEADS):
        sl = slice(h * HEAD_DIM, (h + 1) * HEAD_DIM)
        q = q_ref[:, sl].astype(F32)
        q = _rotate_half(q * _rms_scale(q, HEAD_DIM) * gq_ref[...], cos, sin)
        qo_ref[:, sl] = (q * scale).astype(BF16)
        k = k_ref[:, sl].astype(F32)
        k = _rotate_half(k * _rms_scale(k, HEAD_DIM) * gk_ref[...], cos, sin)
        ko_ref[:, sl] = k.astype(BF16)


def dil_prep(z, gq, gk, cos, sin, tm=512):
    m = z.shape[0]
    sec = lambda c: pl.BlockSpec((tm, GROUP_W), lambda i: (i, c // GROUP_W))
    vec = pl.BlockSpec((1, LANES), lambda i: (0, 0))
    tab = pl.BlockSpec((tm, LANES), lambda i: (i, 0))
    out = pl.BlockSpec((tm, GROUP_W), lambda i: (i, 0))
    shp = jax.ShapeDtypeStruct((m, GROUP_W), BF16)
    return pl.pallas_call(
        _dil_prep_kernel,
        out_shape=(shp, shp),
        grid=(m // tm,),
        in_specs=[sec(COL_DQ), sec(COL_DK), vec, vec, tab, tab],
        out_specs=(out, out),
        compiler_params=_params(("parallel",)),
        name="dil_prep",
    )(z, z, gq, gk, cos, sin)


def _dil_windows(seq, tq):
    return [(dil, window // 2, min(seq, tq + window)) for window, dil in DIL_PAIRS]


def _dil_attn_kernel(*refs, seq, tq):
    q_refs, k_refs, v_refs, o_refs = refs[0:3], refs[3:6], refs[6:9], refs[9:12]
    i = pl.program_id(2)
    plan = []
    for g, (dil, reach, win) in enumerate(_dil_windows(seq, tq)):
        start = jnp.clip(i * tq - reach, 0, seq - win)
        start = pl.multiple_of(start, 64)
        s = lax.dot_general(q_refs[g][...], k_refs[g][pl.ds(start, win), :], (((1,), (1,)), ((), ())),
                            preferred_element_type=F32)
        plan.append((g, dil, reach, win, start, s))
    outs, lses = [], []
    for g, dil, reach, win, start, s in plan:
        diff = (lax.broadcasted_iota(jnp.int32, (tq, win), 1)
                - lax.broadcasted_iota(jnp.int32, (tq, win), 0)) + (start - i * tq)
        valid = jnp.abs(diff) <= reach
        if dil > 1:
            valid = valid & ((diff & (dil - 1)) == 0)
        s = jnp.where(valid, s, NEG)
        m = jnp.max(s, axis=-1, keepdims=True)
        p = jnp.exp2(s - m)
        l = jnp.sum(p, axis=-1, keepdims=True)
        v = v_refs[g][pl.ds(start, win), :]
        outs.append(jnp.dot(p.astype(BF16), v, preferred_element_type=F32) / l)
        lses.append(m + jnp.log2(l))
    top = jnp.maximum(jnp.maximum(lses[0], lses[1]), lses[2])
    es = [jnp.exp2(t - top) for t in lses]
    den = es[0] + es[1] + es[2]
    for g in range(3):
        o_refs[g][...] = (outs[g] * (es[g] / den)).astype(o_refs[g].dtype)


def dil_attention(qd, kd, z, batch, seq, tq=256):
    tq = min(tq, seq)
    nq = seq // tq
    v0 = COL_DV // HEAD_DIM
    qs = [pl.BlockSpec((tq, HEAD_DIM), functools.partial(
        lambda b, s, i, g: (b * nq + i, g * DIL_SLOTS + s), g=g)) for g in range(3)]
    ks = [pl.BlockSpec((seq, HEAD_DIM), functools.partial(
        lambda b, s, i, g: (b, g * DIL_SLOTS + s), g=g)) for g in range(3)]
    vs = [pl.BlockSpec((seq, HEAD_DIM), functools.partial(
        lambda b, s, i, g: (b, v0 + g * DIL_SLOTS + s), g=g)) for g in range(3)]
    os_ = [pl.BlockSpec((tq, HEAD_DIM), lambda b, s, i: (b * nq + i, s)) for _ in range(3)]
    shp = jax.ShapeDtypeStruct((batch * seq, DIL_SLOTS * HEAD_DIM), BF16)
    return pl.pallas_call(
        functools.partial(_dil_attn_kernel, seq=seq, tq=tq),
        out_shape=(shp, shp, shp),
        grid=(batch, DIL_SLOTS, nq),
        in_specs=qs + ks + vs,
        out_specs=tuple(os_),
        compiler_params=_params(("parallel", "parallel", "arbitrary")),
        name="dil_attention",
    )(qd, qd, qd, kd, kd, kd, z, z, z)


def _hy_filter_kernel(z_ref, w1_ref, b1_ref, w2_ref, b2_ref, w3_ref, b3_ref, w4_ref, b4_ref,
                      fq_ref, dl_ref, o_ref, *, tl):
    fq = fq_ref[...]
    h = z_ref[...]
    for w_ref, b_ref in ((w1_ref, b1_ref), (w2_ref, b2_ref), (w3_ref, b3_ref)):
        h = jnp.sin(fq * (jnp.dot(h.astype(BF16), w_ref[...], preferred_element_type=F32) + b_ref[...]))
    h = jnp.dot(h.astype(BF16), w4_ref[...], preferred_element_type=F32) + b4_ref[...]
    t = z_ref[:, 0:1]
    decay = jnp.exp(-t * dl_ref[...])
    o_ref[0] = h[:, :HY_CH] * decay
    n = pl.program_id(0) * tl + lax.broadcasted_iota(jnp.int32, (tl, 1), 0)
    o_ref[1] = jnp.where(n == 0, 0.0, h[:, HY_CH:] * decay)


def hy_filter(zfeat, w1, b1, w2, b2, w3, b3, w4, b4, fq, deltas, tl=512):
    seq = zfeat.shape[0]
    full = lambda a: pl.BlockSpec(a.shape, lambda i: (0,) * a.ndim)
    args = (w1, b1, w2, b2, w3, b3, w4, b4, fq, deltas)
    return pl.pallas_call(
        functools.partial(_hy_filter_kernel, tl=tl),
        out_shape=jax.ShapeDtypeStruct((2, seq, HY_CH), F32),
        grid=(seq // tl,),
        in_specs=[pl.BlockSpec((tl, LANES), lambda i: (i, 0))] + [full(a) for a in args],
        out_specs=pl.BlockSpec((2, tl, HY_CH), lambda i: (0, i, 0)),
        compiler_params=_params(("parallel",)),
        name="hy_filter",
    )(zfeat, *args)


HALO = 16


def _hy_pre_kernel(x0_ref, x1_ref, v_ref, p0_ref, p1_ref, pv_ref, n0_ref, n1_ref, nv_ref,
                   w_ref, b_ref, x0o_ref, u_ref, *, tl, blocks_per_seq):
    i = pl.program_id(0)
    first = (i % blocks_per_seq) == 0
    last = (i % blocks_per_seq) == blocks_per_seq - 1
    row = lax.broadcasted_iota(jnp.int32, (tl, 1), 0)

    def conv(c_ref, p_ref, n_ref, part):
        c = c_ref[...].astype(F32)
        sl = slice(part * HY_CH, (part + 1) * HY_CH)
        prev_row = jnp.where(first, 0.0, p_ref[...].astype(F32)[HALO - 1:HALO, :])
        next_row = jnp.where(last, 0.0, n_ref[...].astype(F32)[0:1, :])
        down = jnp.where(row == 0, prev_row, pltpu.roll(c, 1, axis=0))
        up = jnp.where(row == tl - 1, next_row, pltpu.roll(c, tl - 1, axis=0))
        return down * w_ref[0:1, sl] + c * w_ref[1:2, sl] + up * w_ref[2:3, sl] + b_ref[:, sl]

    x0o_ref[...] = conv(x0_ref, p0_ref, n0_ref, 0)
    u_ref[...] = conv(x1_ref, p1_ref, n1_ref, 1) * conv(v_ref, pv_ref, nv_ref, 2)


def hy_pre(z, conv_w, conv_b, seq, tl=512):
    m = z.shape[0]
    nblk = m // tl
    rh = tl // HALO
    c0 = COL_HY // HY_CH
    cur = lambda p: pl.BlockSpec((tl, HY_CH), lambda i: (i, c0 + p))
    prv = lambda p: pl.BlockSpec((HALO, HY_CH), lambda i: (jnp.maximum(i * rh - 1, 0), c0 + p))
    nxt = lambda p: pl.BlockSpec((HALO, HY_CH),
                                 lambda i: (jnp.minimum((i + 1) * rh, m // HALO - 1), c0 + p))
    out = pl.BlockSpec((tl, HY_CH), lambda i: (i, 0))
    shp = jax.ShapeDtypeStruct((m, HY_CH), F32)
    return pl.pallas_call(
        functools.partial(_hy_pre_kernel, tl=tl, blocks_per_seq=seq // tl),
        out_shape=(shp, shp),
        grid=(nblk,),
        in_specs=[cur(0), cur(1), cur(2), prv(0), prv(1), prv(2), nxt(0), nxt(1), nxt(2),
                  pl.BlockSpec((3, 3 * HY_CH), lambda i: (0, 0)),
                  pl.BlockSpec((1, 3 * HY_CH), lambda i: (0, 0))],
        out_specs=(out, out),
        compiler_params=_params(("parallel",)),
        name="hy_pre",
    )(z, z, z, z, z, z, z, z, z, conv_w, conv_b)


def _buffering(block_bytes):
    return dict(pipeline_mode=pl.Buffered(1)) if block_bytes > 8 * 1024 * 1024 else {}


def _dft_slow_kernel(a_ref, x_ref, o_ref, *, p, ph):
    for q in range(DFT_Q):
        xq = x_ref[pl.ds(q, ph, stride=DFT_Q), :].astype(BF16)
        y = jnp.dot(a_ref[...], xq, preferred_element_type=F32)
        o_ref[0, pl.ds(q, p, stride=DFT_Q), :] = y[:p]
        o_ref[1, pl.ds(q, p, stride=DFT_Q), :] = y[p:]


def dft_slow(a, x, tc):
    bsz, seq, ch = x.shape
    p = a.shape[0] // 2
    n = p * DFT_Q
    return pl.pallas_call(
        functools.partial(_dft_slow_kernel, p=p, ph=p // 2),
        out_shape=jax.ShapeDtypeStruct((bsz, 2, n, ch), F32),
        grid=(bsz, ch // tc),
        in_specs=[pl.BlockSpec(a.shape, lambda b, c: (0, 0)),
                  pl.BlockSpec((None, seq, tc), lambda b, c: (b, 0, c))],
        out_specs=pl.BlockSpec((None, 2, n, tc), lambda b, c: (b, 0, 0, c)),
        compiler_params=_params(("parallel", "parallel")),
        name="dft_slow",
    )(a, x)


def _cmul(ar, ai, br, bi):
    return ar * br - ai * bi, ar * bi + ai * br


def _dft_q(mat_ref, xr, xi):
    y = jnp.dot(mat_ref[...], jnp.concatenate([xr, xi], axis=0).astype(BF16),
                preferred_element_type=F32)
    return y[:DFT_Q], y[DFT_Q:]


SLABS = 4


def _hy_kspec_kernel(g_ref, tw_ref, fwd_ref, o_ref):
    for s in range(SLABS):
        rows = slice(s * DFT_Q, (s + 1) * DFT_Q)
        tr, ti = tw_ref[s, 0], tw_ref[s, 1]
        fr, fi = _dft_q(fwd_ref, *_cmul(g_ref[0, 0, rows, :], g_ref[0, 1, rows, :], tr, ti))
        br, bi = _dft_q(fwd_ref, *_cmul(g_ref[1, 0, rows, :], g_ref[1, 1, rows, :], tr, ti))
        o_ref[0, rows, :] = fr + br
        o_ref[1, rows, :] = fi - bi


def hy_kspec(g, tw, fwd, tc=512):
    n = g.shape[2]
    rows = SLABS * DFT_Q
    return pl.pallas_call(
        _hy_kspec_kernel,
        out_shape=jax.ShapeDtypeStruct((2, n, HY_CH), F32),
        grid=(n // rows, HY_CH // tc),
        in_specs=[pl.BlockSpec((2, 2, rows, tc), lambda kp, c: (0, 0, kp, c)),
                  pl.BlockSpec((SLABS, 2, DFT_Q, 1), lambda kp, c: (kp, 0, 0, 0)),
                  pl.BlockSpec((2 * DFT_Q, 2 * DFT_Q), lambda kp, c: (0, 0))],
        out_specs=pl.BlockSpec((2, rows, tc), lambda kp, c: (0, kp, c)),
        compiler_params=_params(("parallel", "parallel")),
        name="hy_kspec",
    )(g, tw, fwd)


def _hy_mid_kernel(g_ref, tw_ref, fwd_ref, inv_ref, ks_ref, o_ref):
    for s in range(SLABS):
        rows = slice(s * DFT_Q, (s + 1) * DFT_Q)
        tr, ti = tw_ref[s, 0], tw_ref[s, 1]
        xr, xi = _dft_q(fwd_ref, *_cmul(g_ref[0, rows, :], g_ref[1, rows, :], tr, ti))
        yr, yi = _cmul(xr, xi, ks_ref[0, rows, :], ks_ref[1, rows, :])
        wr, wi = _dft_q(inv_ref, yr, yi)
        o_ref[0, rows, :], o_ref[1, rows, :] = _cmul(wr, wi, tr, -ti)


def hy_mid(g, tw, fwd, inv, kspec, tc=512):
    bsz, _, n, _ = g.shape
    rows = SLABS * DFT_Q
    return pl.pallas_call(
        _hy_mid_kernel,
        out_shape=jax.ShapeDtypeStruct(g.shape, F32),
        grid=(n // rows, HY_CH // tc, bsz),
        in_specs=[pl.BlockSpec((None, 2, rows, tc), lambda kp, c, b: (b, 0, kp, c)),
                  pl.BlockSpec((SLABS, 2, DFT_Q, 1), lambda kp, c, b: (kp, 0, 0, 0)),
                  pl.BlockSpec((2 * DFT_Q, 2 * DFT_Q), lambda kp, c, b: (0, 0)),
                  pl.BlockSpec((2 * DFT_Q, 2 * DFT_Q), lambda kp, c, b: (0, 0)),
                  pl.BlockSpec((2, rows, tc), lambda kp, c, b: (0, kp, c))],
        out_specs=pl.BlockSpec((None, 2, rows, tc), lambda kp, c, b: (b, 0, kp, c)),
        compiler_params=_params(("parallel", "parallel", "parallel")),
        name="hy_mid",
    )(g, tw, fwd, inv, kspec)


def _hy_post_kernel(a_ref, h_ref, u_ref, x0_ref, skip_ref, o_ref, *, p, ph, inv_n):
    skip = skip_ref[...]
    for q in range(DFT_Q):
        hq = jnp.concatenate([h_ref[0, pl.ds(q, p, stride=DFT_Q), :],
                              h_ref[1, pl.ds(q, p, stride=DFT_Q), :]], axis=0).astype(BF16)
        y = jnp.dot(a_ref[...], hq, preferred_element_type=F32) * inv_n
        tq = pl.ds(q, ph, stride=DFT_Q)
        o_ref[tq, :] = x0_ref[tq, :] * (y + u_ref[tq, :] * skip)


def hy_post(a, h, u, x0, skip, tc):
    bsz, _, n, ch = h.shape
    seq = n // 2
    p = n // DFT_Q
    seq_spec = pl.BlockSpec((None, seq, tc), lambda b, c: (b, 0, c))
    return pl.pallas_call(
        functools.partial(_hy_post_kernel, p=p, ph=p // 2, inv_n=1.0 / n),
        out_shape=jax.ShapeDtypeStruct((bsz, seq, ch), F32),
        grid=(bsz, ch // tc),
        in_specs=[pl.BlockSpec(a.shape, lambda b, c: (0, 0)),
                  pl.BlockSpec((None, 2, n, tc), lambda b, c: (b, 0, 0, c), **_buffering(2 * n * tc * 4)),
                  seq_spec, seq_spec,
                  pl.BlockSpec((1, tc), lambda b, c: (0, c))],
        out_specs=seq_spec,
        compiler_params=_params(("parallel", "parallel")),
        name="hy_post",
    )(a, h, u, x0, skip)


def _dft_constants(seq):
    n = 2 * seq
    q = DFT_Q
    p = n // q
    ph = p // 2
    fp = np.exp(-2j * np.pi * np.outer(np.arange(p), np.arange(p)) / p)
    fq = np.exp(-2j * np.pi * np.outer(np.arange(q), np.arange(q)) / q)
    tw = np.exp(-2j * np.pi * np.outer(np.arange(p), np.arange(q)) / n)
    a1 = np.concatenate([fp.real[:, :ph], fp.imag[:, :ph]], axis=0)
    fwd = np.block([[fq.real, -fq.imag], [fq.imag, fq.real]])
    inv = np.block([[fq.real, fq.imag], [-fq.imag, fq.real]])
    a3 = np.concatenate([fp.real[:ph, :], fp.imag[:ph, :]], axis=1)
    twa = np.stack([tw.real, tw.imag], axis=1)[..., None]
    cast = lambda a: jnp.asarray(a, F32).astype(BF16)
    return dict(n=n, p=p, ph=ph, a1=cast(a1), fwd=cast(fwd), inv=cast(inv), a3=cast(a3),
                tw=jnp.asarray(twa, F32))


def _hy_features(seq):
    t = jnp.linspace(0.0, 1.0, seq, dtype=F32)[:, None]
    w = 2.0 * math.pi * jnp.arange(seq, dtype=F32)[:, None] / seq
    f = jnp.linspace(1e-4, HY_BANDS - 1, HY_BANDS, dtype=F32)[None, :]
    z = jnp.concatenate([t, jnp.cos(f * w), -jnp.sin(f * w)], axis=-1)
    return jnp.pad(z, ((0, 0), (0, LANES - HY_EMB)))


def _hy_deltas():
    max_decay = math.log(1e-2) / 0.3
    min_decay = math.log(1e-2) / 1.5
    return jnp.abs(jnp.linspace(min_decay, max_decay, HY_CH, dtype=F32))[None, :]


def _pad2(a, rows, cols, value=0.0):
    return jnp.pad(a, ((0, rows - a.shape[0]), (0, cols - a.shape[1])), constant_values=value)


def hyena_spectrum(seq, consts, fw):
    h = hy_filter(_hy_features(seq), *fw, _hy_deltas(), tl=min(512, seq))
    g = dft_slow(consts["a1"], h, HY_SEQ_TILE)
    return hy_kspec(g, consts["tw"], consts["fwd"])


def hyena_conv(x0, u, batch, seq, consts, kspec, skip):
    tc = HY_SEQ_TILE
    uv = u.reshape(batch, seq, HY_CH)
    g = dft_slow(consts["a1"], uv, tc)
    hm = hy_mid(g, consts["tw"], consts["fwd"], consts["inv"], kspec)
    y = hy_post(consts["a3"], hm, uv, x0.reshape(batch, seq, HY_CH), skip, tc)
    return y.reshape(batch * seq, HY_CH)


def _input_projection(w_in):
    hy_cols = 3 * HY_CH
    o_cq = hy_cols
    o_ckv = o_cq + Q_LORA
    o_kr = o_ckv + KV_LORA
    o_dil = o_kr + ROPE_DIM
    return jnp.concatenate([
        w_in[..., o_dil:], w_in[..., o_cq:o_ckv], w_in[..., :hy_cols], w_in[..., o_ckv:o_kr],
        _spread_halves(w_in[..., o_kr:o_dil]),
        jnp.zeros(w_in.shape[:-1] + (Z_COLS - Z_USED,), w_in.dtype)], axis=-1).astype(BF16)


def _layer_params(l, norm_mix, w_in, hy_conv_w, hy_conv_b, hy_f_w1, hy_f_b1, hy_f_w2, hy_f_b2, hy_f_w3,
                  hy_f_b3, hy_f_w4, hy_f_b4, hy_f_freq, hy_skip, mla_q_a_norm, mla_w_q_b, mla_kv_a_norm,
                  mla_w_kv_b, mla_qn_nope, mla_qn_rope, mla_kn_nope, mla_kn_rope, dil_q_norm, dil_k_norm,
                  out_norm, w_out, norm_ffn, w_up, w_down):
    wq = mla_w_q_b[l].reshape(Q_LORA, HEADS, QK_DIM)
    wq = jnp.concatenate([wq[..., :HEAD_DIM], _spread_halves(wq[..., HEAD_DIM:])], axis=-1)
    wide = HEAD_GROUP * 2 * LANES
    wq = wq.reshape(Q_LORA, HEADS // HEAD_GROUP, wide).transpose(1, 0, 2).astype(BF16)
    wkv = mla_w_kv_b[l].reshape(KV_LORA, HEADS // HEAD_GROUP, wide).transpose(1, 0, 2).astype(BF16)
    hf = LANES
    filt = (_pad2(hy_f_w1[l], hf, hf).astype(BF16), _pad2(hy_f_b1[l][None], 1, hf),
            _pad2(hy_f_w2[l], hf, hf).astype(BF16), _pad2(hy_f_b2[l][None], 1, hf),
            _pad2(hy_f_w3[l], hf, hf).astype(BF16), _pad2(hy_f_b3[l][None], 1, hf),
            _pad2(hy_f_w4[l], hf, 2 * HY_CH).astype(BF16), hy_f_b4[l][None],
            _pad2(hy_f_freq[l][None], 1, hf, 1.0))
    return dict(
        norm_mix=norm_mix[l],
        conv_w=hy_conv_w[l], conv_b=hy_conv_b[l][None], filt=filt,
        skip=hy_skip[l][None],
        q_a_norm=mla_q_a_norm[l][None], w_q=wq,
        q_gain=jnp.concatenate([mla_qn_nope[l], _spread_halves(mla_qn_rope[l])])[None],
        kv_a_norm=mla_kv_a_norm[l][None], w_kv=wkv,
        kn_nope=mla_kn_nope[l][None], kn_rope=_spread_halves(mla_kn_rope[l])[None],
        dil_q=dil_q_norm[l][None], dil_k=dil_k_norm[l][None],
        out_norm=out_norm[l], norm_ffn=norm_ffn[l], layer=l)


def mixers(z, p, batch, seq, consts, tabs):
    mla_cos, mla_sin, dil_cos, dil_sin = tabs
    x0, u = hy_pre(z, p["conv_w"], p["conv_b"], seq)
    kspec = hyena_spectrum(seq, consts, p["filt"])
    y_hy = hyena_conv(x0, u, batch, seq, consts, kspec, p["skip"])
    q = mla_q(z, p["q_a_norm"], p["w_q"], p["q_gain"], mla_cos, mla_sin)
    k, v = mla_kv(z, p["kv_a_norm"], p["w_kv"], p["kn_nope"], p["kn_rope"], mla_cos, mla_sin)
    y_mla = mla_attention(q, k, v, batch, seq)
    qd, kd = dil_prep(z, p["dil_q"], p["dil_k"], dil_cos, dil_sin)
    return (y_hy, y_mla) + tuple(dil_attention(qd, kd, z, batch, seq))


def rope_tabs(batch, seq):
    pos = jnp.tile(jnp.arange(seq, dtype=jnp.int32), batch)
    c64, s64 = _rope_tables(pos, ROPE_DIM)
    c128, s128 = _rope_tables(pos, HEAD_DIM)
    return (_spread_halves(jnp.concatenate([c64, c64], -1)),
            _spread_halves(jnp.concatenate([-s64, s64], -1)),
            jnp.concatenate([c128, c128], -1), jnp.concatenate([-s128, s128], -1))


def _trunk(xin, layers, shared):
    batch, seq, d = xin.shape
    x = xin.reshape(batch * seq, d)
    tabs = rope_tabs(batch, seq)
    consts = _dft_constants(seq)
    for p in layers:
        z = norm_matmul(x, p["norm_mix"], shared["w_in"], p["layer"])
        ys = mixers(z, p, batch, seq, consts, tabs)
        x = out_proj(*ys, p["out_norm"], shared["w_out"], x, p["layer"])
        x = ffn(x, p["norm_ffn"], shared["w_up"], shared["w_down"], p["layer"])
    return x.reshape(xin.shape)


def kernel(x_prompt, x_sample, norm_mix, w_in, hy_conv_w, hy_conv_b, hy_f_w1, hy_f_b1, hy_f_w2, hy_f_b2, hy_f_w3, hy_f_b3, hy_f_w4, hy_f_b4, hy_f_freq, hy_skip, mla_q_a_norm, mla_w_q_b, mla_kv_a_norm, mla_w_kv_b, mla_qn_nope, mla_qn_rope, mla_kn_nope, mla_kn_rope, dil_q_norm, dil_k_norm, out_norm, w_out, norm_ffn, w_up, w_down):
    weights = (norm_mix, w_in, hy_conv_w, hy_conv_b, hy_f_w1, hy_f_b1, hy_f_w2, hy_f_b2, hy_f_w3, hy_f_b3,
               hy_f_w4, hy_f_b4, hy_f_freq, hy_skip, mla_q_a_norm, mla_w_q_b, mla_kv_a_norm, mla_w_kv_b,
               mla_qn_nope, mla_qn_rope, mla_kn_nope, mla_kn_rope, dil_q_norm, dil_k_norm, out_norm, w_out,
               norm_ffn, w_up, w_down)
    layers = [_layer_params(l, *weights) for l in range(norm_mix.shape[0])]
    shared = dict(w_in=_input_projection(w_in), w_out=w_out.astype(BF16), w_up=w_up.astype(BF16),
                  w_down=w_down.astype(BF16))
    return tuple(_trunk(xin, layers, shared) for xin in (x_prompt, x_sample))
```
